```python
import math
import jax
import jax.numpy as jnp
from jax import lax
import numpy as np

D_MODEL = 1024
BATCH = 2
SEQ = 8192
DEPTH = 2
DEC_BATCH = 32
DEC_SEQ = 1
PAST_LEN = 8192
PAGE_SIZE = 128

N_A_LAYERS = DEPTH // 2
N_B_LAYERS = DEPTH - N_A_LAYERS
A_HEAD = 64
A_HEADS = D_MODEL // A_HEAD
DECAY_LORA = 64
AAA_LORA = 64
GATE_LORA = 160
GN_EPS = 64e-5
B_HEAD = 64
B_HEADS = D_MODEL // (2 * B_HEAD)
ATTN_SCALE = B_HEAD ** -0.5
Q_BLOCK = 128
SUBLN_EPS = 1e-5
NEG = -1e30
D_FF = 4 * D_MODEL
LN_EPS = 1e-5
DN_ALPHA = (2 * DEPTH) ** 0.25
DN_BETA = (8 * DEPTH) ** -0.25

kernel_name = 'yoco_rwkv7_diffattn_decode_step'


def layer_norm(x, g, b):
    xf = x.astype(jnp.float32)
    mu = jnp.mean(xf, -1, keepdims=True)
    var = jnp.mean(jnp.square(xf - mu), -1, keepdims=True)
    return ((xf - mu) * lax.rsqrt(var + LN_EPS) * g + b).astype(x.dtype)


def ada_mod(c, w, b, n):
    m = jnp.dot(jax.nn.silu(c), w) + b
    return jnp.split(m[:, None, :], n, axis=-1)


def sq_relu_mlp(h, w1, w2):
    return jnp.dot(jnp.square(jax.nn.relu(jnp.dot(h, w1))), w2)


def wkv7_scan(r, w, k, v, a, b, s0):
    def step(S, inp):
        rt, wt, kt, vt, at, bt = inp
        sa = jnp.einsum('bhij,bhj->bhi', S, at)
        S = S * wt[:, :, None, :] + sa[..., None] * bt[:, :, None, :] + vt[..., None] * kt[:, :, None, :]
        return S, jnp.einsum('bhij,bhj->bhi', S, rt)
    seq = tuple(jnp.moveaxis(t.astype(jnp.float32), 1, 0) for t in (r, w, k, v, a, b))
    s_fin, y = lax.scan(step, s0.astype(jnp.float32), seq)
    return jnp.moveaxis(y, 0, 1), s_fin


def rwkv7_mix(h, prev, s0, mu, w_rkv, w_o, w0, w1, w2, a0, a1, a2, g1, g2, k_k, k_a, r_k, gn_g, gn_b):
    Bn, T, D = h.shape
    h_prev = jnp.concatenate([prev[:, None, :].astype(h.dtype), h[:, :-1]], axis=1)
    xx = h_prev - h
    xs = h[None] + xx[None] * mu[:, None, None, :]
    r, k, v = jnp.einsum('pbtd,pde->pbte', xs[:3], w_rkv)
    xw, xa, xg = xs[3], xs[4], xs[5]
    w_log = -jax.nn.softplus(-(w0 + jnp.dot(jnp.tanh(jnp.dot(xw, w1)), w2))) - 0.5
    decay = jnp.exp(-jnp.exp(w_log.astype(jnp.float32)))
    a = jax.nn.sigmoid(a0 + jnp.dot(jnp.dot(xa, a1), a2))
    g = jnp.dot(jax.nn.sigmoid(jnp.dot(xg, g1)), g2)
    split = lambda t: t.reshape(Bn, T, A_HEADS, A_HEAD)
    kkf = split(k * k_k).astype(jnp.float32)
    kk = kkf / jnp.maximum(jnp.linalg.norm(kkf, axis=-1, keepdims=True), 1e-12)
    k = k * (1 + (a - 1) * k_a)
    r_h, k_h, v_h, a_h = split(r), split(k), split(v), split(a)
    y, s_fin = wkv7_scan(r_h, split(decay), k_h, v_h, -kk, kk * a_h, s0)
    mu_y = jnp.mean(y, -1, keepdims=True)
    var_y = jnp.mean(jnp.square(y - mu_y), -1, keepdims=True)
    yn = ((y - mu_y) * lax.rsqrt(var_y + GN_EPS)).reshape(Bn, T, D) * gn_g + gn_b
    bonus = jnp.sum(r_h * k_h * r_k, -1, keepdims=True) * v_h
    out = (yn + bonus.reshape(Bn, T, D)) * g
    return jnp.dot(out.astype(h.dtype), w_o), h[:, -1], s_fin


def diff_lambda(lam_q, lam_k, lam_init):
    lq = lam_q.astype(jnp.float32)
    lk = lam_k.astype(jnp.float32)
    return jnp.exp(jnp.sum(lq[0] * lk[0])) - jnp.exp(jnp.sum(lq[1] * lk[1])) + lam_init


def diff_combine(s, lam):
    p = jax.nn.softmax(s, axis=-1)
    return p[..., 0, :, :] - lam * p[..., 1, :, :]


def diff_attn_prompt(q, k, v, lam):
    Bn, S = q.shape[:2]
    nb = S // Q_BLOCK
    qb = q.reshape(Bn, nb, Q_BLOCK, B_HEADS, 2, B_HEAD).swapaxes(0, 1)
    k_pos = jnp.arange(S)

    def block(args):
        i, qi = args
        s = jnp.einsum('bqhmd,bkhmd->bhmqk', qi, k).astype(jnp.float32) * ATTN_SCALE
        q_pos = i * Q_BLOCK + jnp.arange(Q_BLOCK)
        s = jnp.where(k_pos[None, :] <= q_pos[:, None], s, NEG)
        attn = diff_combine(s, lam).astype(v.dtype)
        return jnp.einsum('bhqk,bkhe->bqhe', attn, v)

    o = lax.map(block, (jnp.arange(nb), qb))
    return o.swapaxes(0, 1).reshape(Bn, S, B_HEADS, 2 * B_HEAD)


def diff_attn_sample(q, k_new, v_new, k_past, v_past, lam):
    T = q.shape[1]
    P = k_past.shape[1]
    s_past = jnp.einsum('bqhmd,bkhmd->bhmqk', q, k_past).astype(jnp.float32) * ATTN_SCALE
    s_new = jnp.einsum('bqhmd,bkhmd->bhmqk', q, k_new).astype(jnp.float32) * ATTN_SCALE
    causal = jnp.arange(T)[None, :] <= jnp.arange(T)[:, None]
    s_new = jnp.where(causal, s_new, NEG)
    attn = diff_combine(jnp.concatenate([s_past, s_new], axis=-1), lam).astype(v_new.dtype)
    return (jnp.einsum('bhqk,bkhe->bqhe', attn[..., :P], v_past)
            + jnp.einsum('bhqk,bkhe->bqhe', attn[..., P:], v_new))


def diff_head_out(o, g, lam_init, w_o):
    of = o.astype(jnp.float32)
    on = of * lax.rsqrt(jnp.mean(jnp.square(of), -1, keepdims=True) + SUBLN_EPS) * g * (1.0 - lam_init)
    Bn, T = o.shape[:2]
    return jnp.dot(on.reshape(Bn, T, -1).astype(o.dtype), w_o)


def setup_inputs(seed: int = 0) -> dict:
    key = jax.random.key(seed)
    ks = iter(jax.random.split(key, 64))
    f32 = jnp.float32
    D = D_MODEL
    NA, NB = N_A_LAYERS, N_B_LAYERS

    def nrm(shape, scale=1.0):
        return jax.random.normal(next(ks), shape, f32) * scale

    def unif(shape, lo, hi):
        return jax.random.uniform(next(ks), shape, f32, lo, hi)

    n_pages = PAST_LEN // PAGE_SIZE
    n_used = DEC_BATCH * n_pages
    n_pool = n_used + max(1, n_used // 4)
    perm = jax.random.permutation(next(ks), n_pool)
    page_table = perm[:n_used].reshape(DEC_BATCH, n_pages).astype(jnp.int32)
    sd = D ** -0.5
    rkv_scale = jnp.array([1.0, 1.0, DN_BETA], f32)[None, :, None, None]
    return {
        'x_prompt': nrm((BATCH, SEQ, D)),
        'x_sample': nrm((DEC_BATCH, DEC_SEQ, D)),
        'cache_k': nrm((n_pool, PAGE_SIZE, B_HEADS, 2, B_HEAD)),
        'cache_v': nrm((n_pool, PAGE_SIZE, B_HEADS, 2 * B_HEAD)),
        'state_wkv': nrm((NA, DEC_BATCH, A_HEADS, A_HEAD, A_HEAD), 0.1),
        'state_shift': nrm((NA, DEC_BATCH, D)),
        'page_table': page_table,
        'c_prompt': nrm((BATCH, D)),
        'c_sample': nrm((DEC_BATCH, D)),
        'ln_g': 1.0 + nrm((DEPTH, 2, D), 0.02),
        'ln_b': nrm((DEPTH, 2, D), 0.02),
        'ada_w': nrm((DEPTH, D, 6 * D), 0.1 * sd),
        'ada_b': nrm((DEPTH, 6 * D), 0.01),
        'ffn_w1': nrm((DEPTH, D, D_FF), sd * DN_BETA),
        'ffn_w2': nrm((DEPTH, D_FF, D), D_FF ** -0.5 * DN_BETA),
        'a_mu': unif((NA, 6, D), 0.0, 1.0),
        'a_w_rkv': nrm((NA, 3, D, D), sd) * rkv_scale,
        'a_w_o': nrm((NA, D, D), sd * DN_BETA),
        'a_w0': unif((NA, D), -6.0, 1.0),
        'a_w1': nrm((NA, D, DECAY_LORA), sd),
        'a_w2': nrm((NA, DECAY_LORA, D), 0.1 * DECAY_LORA ** -0.5),
        'a_a0': nrm((NA, D), 0.1),
        'a_a1': nrm((NA, D, AAA_LORA), sd),
        'a_a2': nrm((NA, AAA_LORA, D), 0.1 * AAA_LORA ** -0.5),
        'a_g1': nrm((NA, D, GATE_LORA), sd),
        'a_g2': nrm((NA, GATE_LORA, D), GATE_LORA ** -0.5),
        'a_k_k': 0.85 + nrm((NA, D), 0.02),
        'a_k_a': 1.0 + nrm((NA, D), 0.02),
        'a_r_k': nrm((NA, A_HEADS, A_HEAD), 0.1),
        'a_gn_g': 1.0 + nrm((NA, D), 0.02),
        'a_gn_b': nrm((NA, D), 0.02),
        'kv_ada_w': nrm((D, 2 * D), 0.1 * sd),
        'kv_ada_b': nrm((2 * D,), 0.01),
        'kv_w_k': nrm((D, D), sd),
        'kv_w_v': nrm((D, D), sd * DN_BETA),
        'b_w_q': nrm((NB, D, D), sd),
        'b_w_o': nrm((NB, D, D), sd * DN_BETA),
        'b_lam_q': nrm((NB, 2, B_HEAD), 0.1),
        'b_lam_k': nrm((NB, 2, B_HEAD), 0.1),
        'b_subln_g': 1.0 + nrm((NB, 2 * B_HEAD), 0.02),
    }


def reference(x_prompt, x_sample, cache_k, cache_v, state_wkv, state_shift, page_table, c_prompt, c_sample,
              ln_g, ln_b, ada_w, ada_b, ffn_w1, ffn_w2,
              a_mu, a_w_rkv, a_w_o, a_w0, a_w1, a_w2, a_a0, a_a1, a_a2, a_g1, a_g2,
              a_k_k, a_k_a, a_r_k, a_gn_g, a_gn_b,
              kv_ada_w, kv_ada_b, kv_w_k, kv_w_v,
              b_w_q, b_w_o, b_lam_q, b_lam_k, b_subln_g):

    def trunk(x, c, shift0, wkv0, attend):
        Bn, T, _ = x.shape
        shifts, wkvs = [], []
        k_sh = v_sh = None
        for l in range(DEPTH):
            sh_m, sc_m, gt_m, sh_f, sc_f, gt_f = ada_mod(c, ada_w[l], ada_b[l], 6)
            h = x * (1 + sc_m) + sh_m
            if l < N_A_LAYERS:
                j = l
                mix, last, s_fin = rwkv7_mix(h, shift0[j], wkv0[j], a_mu[j], a_w_rkv[j], a_w_o[j],
                                             a_w0[j], a_w1[j], a_w2[j], a_a0[j], a_a1[j], a_a2[j],
                                             a_g1[j], a_g2[j], a_k_k[j], a_k_a[j], a_r_k[j],
                                             a_gn_g[j], a_gn_b[j])
                shifts.append(last)
                wkvs.append(s_fin)
            else:
                j = l - N_A_LAYERS
                lam_init = 0.8 - 0.6 * math.exp(-0.3 * l)
                lam = diff_lambda(b_lam_q[j], b_lam_k[j], lam_init)
                q = jnp.dot(h, b_w_q[j]).reshape(Bn, T, B_HEADS, 2, B_HEAD)
                o = attend(q, k_sh, v_sh, lam)
                mix = diff_head_out(o, b_subln_g[j], lam_init, b_w_o[j])
            x = layer_norm(DN_ALPHA * x + (1 + gt_m) * mix, ln_g[l, 0], ln_b[l, 0])
            hf = x * (1 + sc_f) + sh_f
            x = layer_norm(DN_ALPHA * x + (1 + gt_f) * sq_relu_mlp(hf, ffn_w1[l], ffn_w2[l]),
                           ln_g[l, 1], ln_b[l, 1])
            if l == N_A_LAYERS - 1:
                sh_kv, sc_kv = ada_mod(c, kv_ada_w, kv_ada_b, 2)
                hkv = x * (1 + sc_kv) + sh_kv
                k_sh = jnp.dot(hkv, kv_w_k).reshape(Bn, T, B_HEADS, 2, B_HEAD)
                v_sh = jnp.dot(hkv, kv_w_v).reshape(Bn, T, B_HEADS, 2 * B_HEAD)
        return x, k_sh, v_sh, jnp.stack(wkvs), jnp.stack(shifts)

    bp = x_prompt.shape[0]
    shift0_p = jnp.zeros((N_A_LAYERS, bp, D_MODEL), x_prompt.dtype)
    wkv0_p = jnp.zeros((N_A_LAYERS, bp, A_HEADS, A_HEAD, A_HEAD), jnp.float32)
    y_prompt, k_prompt, v_prompt, wkv_prompt, shift_prompt = trunk(
        x_prompt, c_prompt, shift0_p, wkv0_p, diff_attn_prompt)

    bd, n_pages = page_table.shape
    k_past = cache_k[page_table].reshape(bd, n_pages * PAGE_SIZE, B_HEADS, 2, B_HEAD)
    v_past = cache_v[page_table].reshape(bd, n_pages * PAGE_SIZE, B_HEADS, 2 * B_HEAD)

    def attend_sample(q, k_new, v_new, lam):
        return diff_attn_sample(q, k_new, v_new, k_past, v_past, lam)

    y_sample, k_sample, v_sample, wkv_sample, shift_sample = trunk(
        x_sample, c_sample, state_shift, state_wkv, attend_sample)

    return (y_prompt, y_sample, k_prompt, v_prompt, wkv_prompt, shift_prompt,
            k_sample, v_sample, wkv_sample, shift_sample)
```

```python
import functools
import math

import jax
import jax.numpy as jnp
from jax import lax
from jax.experimental import pallas as pl
from jax.experimental.pallas import tpu as pltpu

F32 = jnp.float32
BF16 = jnp.bfloat16

LANES = 128
A_HEAD = 64
B_HEAD = 64
PAGE_SIZE = 128
GN_EPS = 64e-5
SUBLN_EPS = 1e-5
LN_EPS = 1e-5
NEG = -1e30
VMEM_LIMIT = 56 * 2**20


def _params(sem):
    return pltpu.CompilerParams(dimension_semantics=sem, vmem_limit_bytes=VMEM_LIMIT)


def _dot(a, b):
    return jnp.dot(a, b, preferred_element_type=F32)


def _dot_nt(a, b):
    return lax.dot_general(a, b, (((1,), (1,)), ((), ())), preferred_element_type=F32)


def _split(x):
    hi = x.astype(BF16)
    lo = (x - hi.astype(F32)).astype(BF16)
    return hi, lo


def _sigmoid(z):
    return 1.0 / (1.0 + jnp.exp(-z))


def _layer_norm(z, g, b):
    mu = jnp.mean(z, -1, keepdims=True)
    d = z - mu
    var = jnp.mean(d * d, -1, keepdims=True)
    return d * lax.rsqrt(var + LN_EPS) * g + b


def _pair_ones():
    r = lax.broadcasted_iota(jnp.int32, (LANES, LANES), 0) >> 6
    c = lax.broadcasted_iota(jnp.int32, (LANES, LANES), 1) >> 6
    return jnp.where(r == c, 1.0, 0.0).astype(BF16)


def _head_sum(x, ones):
    m, d = x.shape
    cols = d // LANES
    stacked = jnp.concatenate([x[:, c * LANES:(c + 1) * LANES] for c in range(cols)], axis=0)
    hi, lo = _split(stacked)
    s = _dot(hi, ones) + _dot(lo, ones)
    return jnp.concatenate([s[c * m:(c + 1) * m] for c in range(cols)], axis=1)


def _mod_spec(mod, tiles_per_batch):
    return pl.BlockSpec((1, mod.shape[1], mod.shape[2]), lambda i, *_: (i // tiles_per_batch, 0, 0))


def _full(a):
    nd = a.ndim
    return pl.BlockSpec(a.shape, lambda *_: (0,) * nd)


def _ada_kernel(c_ref, w_ref, b_ref, o_ref):
    c = c_ref[...]
    s = c * _sigmoid(c)
    o_ref[...] = _dot(s.astype(BF16), w_ref[...].astype(BF16)) + b_ref[...]


def _ada(c, w, b):
    m, d = c.shape
    n = w.shape[1]
    tn = 1024
    return pl.pallas_call(
        _ada_kernel,
        out_shape=jax.ShapeDtypeStruct((m, n), F32),
        grid=(n // tn,),
        in_specs=[pl.BlockSpec((m, d), lambda j: (0, 0)),
                  pl.BlockSpec((d, tn), lambda j: (0, j)),
                  pl.BlockSpec((1, tn), lambda j: (0, j))],
        out_specs=pl.BlockSpec((m, tn), lambda j: (0, j)),
        compiler_params=_params(("arbitrary",)),
        name="ada_proj",
    )(c, w, b.reshape(1, n))


def _rwkv_proj_kernel(single_step, tiles_per_batch, d,
                      x_ref, xp_ref, mod_ref, sh0_ref, mu_ref, wrkv_ref, w1_ref, w2_ref, a1_ref, a2_ref,
                      g1_ref, g2_ref, vec_ref,
                      r_o, w_o, k_o, v_o, a_o, b_o, g_o, bonus_o, last_o):
    mod = mod_ref[0]
    sh = mod[:, 0:d]
    sc = mod[:, d:2 * d]
    x = x_ref[...]
    h = x * (1.0 + sc) + sh
    if single_step:
        h_prev = sh0_ref[0]
        last_o[0] = h
    else:
        i = pl.program_id(0)
        prev_row = xp_ref[7:8, :] * (1.0 + sc) + sh
        first = jnp.where(i % tiles_per_batch == 0, sh0_ref[0], prev_row)
        rolled = pltpu.roll(h, 1, 0)
        rid = lax.broadcasted_iota(jnp.int32, h.shape, 0)
        h_prev = jnp.where(rid == 0, first, rolled)
        last_o[0] = h[h.shape[0] - 1:, :]
    xx = h_prev - h

    def mix(p):
        return (h + xx * mu_ref[p:p + 1, :]).astype(BF16)

    r = _dot(mix(0), wrkv_ref[0])
    k = _dot(mix(1), wrkv_ref[1])
    v = _dot(mix(2), wrkv_ref[2])
    lw = _dot(jnp.tanh(_dot(mix(3), w1_ref[...])).astype(BF16), w2_ref[...])
    la = _dot(_dot(mix(4), a1_ref[...]).astype(BF16), a2_ref[...])
    g = _dot(_sigmoid(_dot(mix(5), g1_ref[...])).astype(BF16), g2_ref[...])

    w0 = vec_ref[0:1, :]
    a0 = vec_ref[1:2, :]
    k_k = vec_ref[2:3, :]
    k_a = vec_ref[3:4, :]
    r_k = vec_ref[4:5, :]

    z = -(w0 + lw)
    softplus = jnp.maximum(z, 0.0) + jnp.log(1.0 + jnp.exp(-jnp.abs(z)))
    w_log = -softplus - 0.5
    decay = jnp.exp(-jnp.exp(w_log))
    a_gate = _sigmoid(a0 + la)
    ones = _pair_ones()
    kkf = k * k_k
    norm = jnp.sqrt(_head_sum(kkf * kkf, ones))
    kk = kkf / jnp.maximum(norm, 1e-12)
    k_mod = k * (1.0 + (a_gate - 1.0) * k_a)
    bonus = _head_sum(r * k_mod * r_k, ones) * v

    r_o[...] = r
    w_o[...] = decay
    k_o[...] = k_mod
    v_o[...] = v
    a_o[...] = -kk
    b_o[...] = kk * a_gate
    g_o[...] = g
    bonus_o[...] = bonus


def _rwkv_proj(x2d, mod, shift0, seq_len, tm, mu, wrkv, w1, w2, a1, a2, g1, g2, vec):
    n, d = x2d.shape
    single = seq_len == 1
    tiles_per_batch = 1 if single else seq_len // tm
    nt = n // tm
    nb = mod.shape[0]
    last_rows = tm if single else 1
    row = pl.BlockSpec((tm, d), lambda i: (i, 0))
    if single:
        xp_spec = pl.BlockSpec((8, d), lambda i: (0, 0))
        sh0_spec = pl.BlockSpec((1, tm, d), lambda i: (i, 0, 0))
    else:
        xp_spec = pl.BlockSpec((8, d), lambda i: (jnp.maximum(i * (tm // 8) - 1, 0), 0))
        sh0_spec = pl.BlockSpec((1, 1, d), lambda i: (i // tiles_per_batch, 0, 0))
    out_sds = jax.ShapeDtypeStruct((n, d), F32)
    outs = pl.pallas_call(
        functools.partial(_rwkv_proj_kernel, single, tiles_per_batch, d),
        out_shape=[out_sds] * 8 + [jax.ShapeDtypeStruct((nb, last_rows, d), F32)],
        grid=(nt,),
        in_specs=[row, xp_spec, _mod_spec(mod, tiles_per_batch), sh0_spec, _full(mu), _full(wrkv), _full(w1),
                  _full(w2), _full(a1), _full(a2), _full(g1), _full(g2), _full(vec)],
        out_specs=[row] * 8 + [pl.BlockSpec((1, last_rows, d), lambda i: (i // tiles_per_batch, 0, 0))],
        compiler_params=_params(("arbitrary",)),
        name="rwkv_proj",
    )(x2d, x2d, mod, shift0, mu, wrkv, w1, w2, a1, a2, g1, g2, vec)
    return outs


def _wkv_kernel(tc, r_ref, w_ref, k_ref, v_ref, a_ref, b_ref, s0_ref, y_ref, s_ref):
    @pl.when(pl.program_id(1) == 0)
    def _():
        s_ref[...] = s0_ref[...]

    groups, pairs = s_ref.shape[0], s_ref.shape[1]
    ones = _pair_ones()
    rid = lax.broadcasted_iota(jnp.int32, (A_HEAD, LANES), 0)
    cid = lax.broadcasted_iota(jnp.int32, (A_HEAD, LANES), 1) & (A_HEAD - 1)
    diag = rid == cid
    idx = [(g, p) for g in range(groups) for p in range(pairs)]
    n = len(idx)

    rb = min(8, tc)
    sub = lax.broadcasted_iota(jnp.int32, (rb, LANES), 0)

    def block(tb, carry):
        base = pl.multiple_of(tb * rb, rb)

        def tile(ref, g, p):
            return ref[g, pl.ds(base, rb), p * LANES:(p + 1) * LANES]

        r_t, w_t, k_t, v_t, a_t, b_t = ([tile(ref, g, p) for g, p in idx]
                                        for ref in (r_ref, w_ref, k_ref, v_ref, a_ref, b_ref))
        states = [s_ref[g, p] for g, p in idx]
        y_tiles = [jnp.zeros((rb, LANES), F32) for _ in idx]
        for i in range(rb):
            row = lambda tiles, j: tiles[j][i:i + 1, :]
            sa_in = jnp.concatenate([(states[j] * row(a_t, j)).astype(BF16) for j in range(n)], axis=0)
            vd = jnp.concatenate([jnp.where(diag, row(v_t, j), 0.0) for j in range(n)], axis=0)
            vd_hi, vd_lo = _split(vd)
            sa = _dot(sa_in, ones)
            vb = _dot(vd_hi, ones) + _dot(vd_lo, ones)
            for j in range(n):
                sl = slice(j * A_HEAD, (j + 1) * A_HEAD)
                states[j] = states[j] * row(w_t, j) + sa[sl] * row(b_t, j) + vb[sl] * row(k_t, j)
            y_in = jnp.concatenate([(states[j] * row(r_t, j)).astype(BF16) for j in range(n)], axis=0)
            yb = _dot(y_in, ones)
            for j in range(n):
                sl = slice(j * A_HEAD, (j + 1) * A_HEAD)
                y_row = jnp.sum(jnp.where(diag, yb[sl], 0.0), axis=0, keepdims=True)
                y_tiles[j] = jnp.where(sub == i, y_row, y_tiles[j])
        for j, (g, p) in enumerate(idx):
            s_ref[g, p] = states[j]
            y_ref[g, pl.ds(base, rb), p * LANES:(p + 1) * LANES] = y_tiles[j]
        return carry

    lax.fori_loop(0, tc // rb, block, 0)


def _wkv(seqs, s0, tc, group):
    b, t, d = seqs[0].shape
    pairs = s0.shape[1]
    seq_spec = pl.BlockSpec((group, tc, d), lambda i, j: (i, j, 0))
    st_spec = pl.BlockSpec((group, pairs, A_HEAD, LANES), lambda i, j: (i, 0, 0, 0))
    return pl.pallas_call(
        functools.partial(_wkv_kernel, tc),
        out_shape=[jax.ShapeDtypeStruct((b, t, d), F32), jax.ShapeDtypeStruct(s0.shape, F32)],
        grid=(b // group, t // tc),
        in_specs=[seq_spec] * 6 + [st_spec],
        out_specs=[seq_spec, st_spec],
        compiler_params=_params(("arbitrary", "arbitrary")),
        name="wkv7_scan",
    )(*seqs, s0)


def _mix_out_kernel(rwkv, alpha, d, *refs):
    if rwkv:
        y_ref, g_ref, bonus_ref, x_ref, mod_ref, wo_ref, gn_ref, ln_ref, o_ref = refs
        y = y_ref[...]
        ones = _pair_ones()
        mu_y = _head_sum(y, ones) * (1.0 / A_HEAD)
        dy = y - mu_y
        var_y = _head_sum(dy * dy, ones) * (1.0 / A_HEAD)
        yn = dy * lax.rsqrt(var_y + GN_EPS) * gn_ref[0:1, :] + gn_ref[1:2, :]
        pre = ((yn + bonus_ref[...]) * g_ref[...]).astype(BF16)
    else:
        on_ref, x_ref, mod_ref, wo_ref, ln_ref, o_ref = refs
        pre = on_ref[...].astype(BF16)
    mix = _dot(pre, wo_ref[...])
    gt = mod_ref[0][:, 2 * d:3 * d]
    z = alpha * x_ref[...] + (1.0 + gt) * mix
    o_ref[...] = _layer_norm(z, ln_ref[0:1, :], ln_ref[1:2, :])


def _mix_out(rwkv, alpha, ins, x2d, mod, tiles_per_batch, tm, wo, gn, ln):
    n, d = x2d.shape
    row = pl.BlockSpec((tm, d), lambda i: (i, 0))
    extra = [gn] if rwkv else []
    args = list(ins) + [x2d, mod, wo] + extra + [ln]
    specs = [row] * (len(ins) + 1) + [_mod_spec(mod, tiles_per_batch), _full(wo)] + [_full(e) for e in extra] + [_full(ln)]
    return pl.pallas_call(
        functools.partial(_mix_out_kernel, rwkv, alpha, d),
        out_shape=jax.ShapeDtypeStruct((n, d), F32),
        grid=(n // tm,),
        in_specs=specs,
        out_specs=row,
        compiler_params=_params(("arbitrary",)),
        name="rwkv_out_ln" if rwkv else "attn_out_ln",
    )(*args)


def _mlp_kernel(alpha, d, x_ref, mod_ref, w1_ref, w2_ref, ln_ref, o_ref, hf_ref, acc_ref):
    j = pl.program_id(1)
    mod = mod_ref[0]

    @pl.when(j == 0)
    def _():
        hf_ref[...] = (x_ref[...] * (1.0 + mod[:, 4 * d:5 * d]) + mod[:, 3 * d:4 * d]).astype(BF16)
        acc_ref[...] = jnp.zeros_like(acc_ref)

    h1 = jnp.maximum(_dot(hf_ref[...], w1_ref[...]), 0.0)
    acc_ref[...] += _dot((h1 * h1).astype(BF16), w2_ref[...])

    @pl.when(j == pl.num_programs(1) - 1)
    def _():
        z = alpha * x_ref[...] + (1.0 + mod[:, 5 * d:6 * d]) * acc_ref[...]
        o_ref[...] = _layer_norm(z, ln_ref[0:1, :], ln_ref[1:2, :])


def _mlp(alpha, x2d, mod, tiles_per_batch, tm, w1, w2, ln):
    n, d = x2d.shape
    dff = w1.shape[1]
    tf = 512
    row = pl.BlockSpec((tm, d), lambda i, j: (i, 0))
    return pl.pallas_call(
        functools.partial(_mlp_kernel, alpha, d),
        out_shape=jax.ShapeDtypeStruct((n, d), F32),
        grid=(n // tm, dff // tf),
        in_specs=[row, _mod_spec(mod, tiles_per_batch),
                  pl.BlockSpec((d, tf), lambda i, j: (0, j)),
                  pl.BlockSpec((tf, d), lambda i, j: (j, 0)),
                  _full(ln)],
        out_specs=row,
        scratch_shapes=[pltpu.VMEM((tm, d), BF16), pltpu.VMEM((tm, d), F32)],
        compiler_params=_params(("arbitrary", "arbitrary")),
        name="sq_relu_mlp_ln",
    )(x2d, mod, w1, w2, ln)


def _qkv_kernel(d, scale, x_ref, kvmod_ref, mod_ref, wk_ref, wv_ref, wq_ref, k_o, v_o, kb_o, vb_o, q_o):
    x = x_ref[...]
    kvmod = kvmod_ref[0]
    hkv = (x * (1.0 + kvmod[:, d:2 * d]) + kvmod[:, 0:d]).astype(BF16)
    k = _dot(hkv, wk_ref[...])
    v = _dot(hkv, wv_ref[...])
    k_o[...] = k
    v_o[...] = v
    kb_o[...] = k.astype(BF16)
    vb_o[...] = v.astype(BF16)
    mod = mod_ref[0]
    h = (x * (1.0 + mod[:, d:2 * d]) + mod[:, 0:d]).astype(BF16)
    q_o[...] = (_dot(h, wq_ref[...]) * scale).astype(q_o.dtype)


def _qkv(x2d, kvmod, mod, tiles_per_batch, tm, wk, wv, wq, scale, q_dtype):
    n, d = x2d.shape
    row = pl.BlockSpec((tm, d), lambda i: (i, 0))
    return pl.pallas_call(
        functools.partial(_qkv_kernel, d, scale),
        out_shape=[jax.ShapeDtypeStruct((n, d), F32)] * 2 + [jax.ShapeDtypeStruct((n, d), BF16)] * 2
                  + [jax.ShapeDtypeStruct((n, d), q_dtype)],
        grid=(n // tm,),
        in_specs=[row, _mod_spec(kvmod, tiles_per_batch), _mod_spec(mod, tiles_per_batch), _full(wk), _full(wv), _full(wq)],
        out_specs=[row] * 5,
        compiler_params=_params(("arbitrary",)),
        name="qkv_proj",
    )(x2d, kvmod, mod, wk, wv, wq)


def _diff_lambda(lq_ref, lk_ref, lam_init):
    lq = lq_ref[...]
    lk = lk_ref[...]
    dots = jnp.sum(lq * lk, -1, keepdims=True)
    return jnp.exp(dots[0:1, :]) - jnp.exp(dots[1:2, :]) + lam_init


def _attn_kernel(tq, lam_init, q_ref, k_ref, v_ref, lq_ref, lk_ref, g_ref, o_ref, qs_ref, m_ref, l_ref, acc_ref):
    qi = pl.program_id(2)
    ki = pl.program_id(3)

    @pl.when(ki == 0)
    def _():
        q = q_ref[0]
        lane = lax.broadcasted_iota(jnp.int32, q.shape, 1)
        qs_ref[0:tq, :] = jnp.where(lane < B_HEAD, q, jnp.zeros_like(q))
        qs_ref[tq:2 * tq, :] = jnp.where(lane >= B_HEAD, q, jnp.zeros_like(q))
        m_ref[...] = jnp.full(m_ref.shape, NEG, F32)
        l_ref[...] = jnp.zeros_like(l_ref)
        acc_ref[...] = jnp.zeros_like(acc_ref)

    @pl.when(ki <= qi)
    def _():
        s = _dot_nt(qs_ref[...], k_ref[0])
        row = lax.broadcasted_iota(jnp.int32, s.shape, 0) & (tq - 1)
        col = lax.broadcasted_iota(jnp.int32, s.shape, 1)
        s = jnp.where(ki * tq + col <= qi * tq + row, s, NEG)
        m_prev = m_ref[...]
        m_new = jnp.maximum(m_prev, jnp.max(s, -1, keepdims=True))
        alpha = jnp.exp(m_prev - m_new)
        p = jnp.exp(s - m_new)
        l_ref[...] = alpha * l_ref[...] + jnp.sum(p, -1, keepdims=True)
        acc_ref[...] = alpha * acc_ref[...] + _dot(p.astype(BF16), v_ref[0])
        m_ref[...] = m_new

    @pl.when(ki == qi)
    def _():
        o_all = acc_ref[...] / l_ref[...]
        lam = _diff_lambda(lq_ref, lk_ref, lam_init)
        o = o_all[0:tq] - lam * o_all[tq:2 * tq]
        on = o * lax.rsqrt(jnp.mean(o * o, -1, keepdims=True) + SUBLN_EPS) * g_ref[...] * (1.0 - lam_init)
        o_ref[0] = on.astype(o_ref.dtype)


def _diff_attn_prompt(q, k, v, lam_q, lam_k, g, lam_init, tq):
    b, s, d = q.shape
    heads = d // LANES
    nq = s // tq
    q_spec = pl.BlockSpec((1, tq, LANES), lambda bi, h, qi, ki: (bi, qi, h))
    kv_spec = pl.BlockSpec((1, tq, LANES), lambda bi, h, qi, ki: (bi, jnp.minimum(ki, qi), h))
    return pl.pallas_call(
        functools.partial(_attn_kernel, tq, lam_init),
        out_shape=jax.ShapeDtypeStruct((b, s, d), BF16),
        grid=(b, heads, nq, nq),
        in_specs=[q_spec, kv_spec, kv_spec, _full(lam_q), _full(lam_k), _full(g)],
        out_specs=q_spec,
        scratch_shapes=[pltpu.VMEM((2 * tq, LANES), BF16), pltpu.VMEM((2 * tq, 1), F32),
                        pltpu.VMEM((2 * tq, 1), F32), pltpu.VMEM((2 * tq, LANES), F32)],
        compiler_params=_params(("arbitrary",) * 4),
        name="diff_attn_prompt",
    )(q, k, v, lam_q, lam_k, g)


def _decode_kernel(pp, lam_init, d, pt_ref, q_ref, kn_ref, vn_ref, lq_ref, lk_ref, g_ref, *rest):
    k_refs = rest[:pp]
    v_refs = rest[pp:2 * pp]
    o_ref, m_ref, l_ref, acc_ref = rest[2 * pp:]
    step = pl.program_id(1)
    heads = d // LANES
    rows = 2 * heads

    @pl.when(step == 0)
    def _():
        m_ref[...] = jnp.full(m_ref.shape, NEG, F32)
        l_ref[...] = jnp.zeros_like(l_ref)
        acc_ref[...] = jnp.zeros_like(acc_ref)

    r = lax.broadcasted_iota(jnp.int32, (rows, d), 0)
    c = lax.broadcasted_iota(jnp.int32, (rows, d), 1)
    sel = ((r & (heads - 1)) == (c >> 7)) & ((r >> 3) == ((c >> 6) & 1))
    sel_b = jnp.where(sel, 1.0, 0.0).astype(BF16)
    q = q_ref[0]

    for i in range(pp):
        kq = (k_refs[i][0] * q).astype(BF16)
        s = _dot_nt(sel_b, kq)
        m_prev = m_ref[...]
        m_new = jnp.maximum(m_prev, jnp.max(s, -1, keepdims=True))
        alpha = jnp.exp(m_prev - m_new)
        p = jnp.exp(s - m_new)
        l_ref[...] = alpha * l_ref[...] + jnp.sum(p, -1, keepdims=True)
        acc_ref[...] = alpha * acc_ref[...] + _dot(p.astype(BF16), v_refs[i][0].astype(BF16))
        m_ref[...] = m_new

    @pl.when(step == pl.num_programs(1) - 1)
    def _():
        kq_new = (kn_ref[0] * q).astype(BF16).astype(F32)
        s_new = jnp.sum(jnp.where(sel, kq_new, 0.0), -1, keepdims=True)
        m_prev = m_ref[...]
        m_fin = jnp.maximum(m_prev, s_new)
        alpha = jnp.exp(m_prev - m_fin)
        p_new = jnp.exp(s_new - m_fin)
        l_fin = alpha * l_ref[...] + p_new
        v_new = vn_ref[0].astype(BF16).astype(F32)
        o_all = (alpha * acc_ref[...] + p_new.astype(BF16).astype(F32) * v_new) / l_fin
        lam = _diff_lambda(lq_ref, lk_ref, lam_init)
        o8 = o_all[0:heads] - lam * o_all[heads:rows]
        hr = lax.broadcasted_iota(jnp.int32, (heads, d), 0)
        hc = lax.broadcasted_iota(jnp.int32, (heads, d), 1) >> 7
        o8 = jnp.where(hr == hc, o8, 0.0)
        ms = jnp.sum(o8 * o8, -1, keepdims=True) * (1.0 / LANES)
        on8 = o8 * lax.rsqrt(ms + SUBLN_EPS)
        on = jnp.sum(on8, axis=0, keepdims=True) * g_ref[...] * (1.0 - lam_init)
        o_ref[0] = on.astype(o_ref.dtype)


def _diff_attn_decode(q, k_new, v_new, cache_k, cache_v, page_table, lam_q, lam_k, g_tiled, lam_init, pp):
    bd, _, d = q.shape
    n_pages = page_table.shape[1]
    row = pl.BlockSpec((1, 1, d), lambda b, s, pt: (b, 0, 0))

    def page_spec(i):
        return pl.BlockSpec((1, PAGE_SIZE, d), lambda b, s, pt: (pt[b, s * pp + i], 0, 0))

    small = lambda a: pl.BlockSpec(a.shape, lambda b, s, pt: (0,) * a.ndim)
    grid_spec = pltpu.PrefetchScalarGridSpec(
        num_scalar_prefetch=1,
        grid=(bd, n_pages // pp),
        in_specs=[row, row, row, small(lam_q), small(lam_k), small(g_tiled)]
                 + [page_spec(i) for i in range(pp)] * 2,
        out_specs=row,
        scratch_shapes=[pltpu.VMEM((2 * d // LANES, 1), F32), pltpu.VMEM((2 * d // LANES, 1), F32),
                        pltpu.VMEM((2 * d // LANES, d), F32)],
    )
    return pl.pallas_call(
        functools.partial(_decode_kernel, pp, lam_init, d),
        out_shape=jax.ShapeDtypeStruct((bd, 1, d), F32),
        grid_spec=grid_spec,
        compiler_params=_params(("arbitrary", "arbitrary")),
        name="diff_attn_decode",
    )(page_table, q, k_new, v_new, lam_q, lam_k, g_tiled, *([cache_k] * pp), *([cache_v] * pp))


def _pad_to(a, axis, size):
    pad = [(0, 0)] * a.ndim
    pad[axis] = (0, size - a.shape[axis])
    return jnp.pad(a, pad)


def _pack_state(s):
    b, h, n, _ = s.shape
    return s.reshape(b, h // 2, 2, n, n).transpose(0, 1, 3, 2, 4).reshape(b, h // 2, n, 2 * n)


def _unpack_state(s):
    b, p, n, _ = s.shape
    return s.reshape(b, p, n, 2, n).transpose(0, 1, 3, 2, 4).reshape(b, 2 * p, n, n)


def kernel(x_prompt, x_sample, cache_k, cache_v, state_wkv, state_shift, page_table, c_prompt, c_sample, ln_g, ln_b, ada_w, ada_b, ffn_w1, ffn_w2, a_mu, a_w_rkv, a_w_o, a_w0, a_w1, a_w2, a_a0, a_a1, a_a2, a_g1, a_g2, a_k_k, a_k_a, a_r_k, a_gn_g, a_gn_b, kv_ada_w, kv_ada_b, kv_w_k, kv_w_v, b_w_q, b_w_o, b_lam_q, b_lam_k, b_subln_g):
    bp, seq, d = x_prompt.shape
    bd, dec_seq, _ = x_sample.shape
    depth = ada_w.shape[0]
    n_a = a_mu.shape[0]
    assert depth == 2 and n_a == 1 and dec_seq == 1
    alpha = (2 * depth) ** 0.25
    heads = d // LANES
    pool = cache_k.shape[0]

    n_c = bp + bd
    c_all = _pad_to(jnp.concatenate([c_prompt, c_sample], axis=0), 0, -(-n_c // 8) * 8)
    mods = [_ada(c_all, ada_w[l], ada_b[l]) for l in range(depth)]
    kvmods = _ada(c_all, kv_ada_w, kv_ada_b)

    wrkv = a_w_rkv[0].astype(BF16)
    lora = 128
    w1 = _pad_to(a_w1[0], 1, lora).astype(BF16)
    w2 = _pad_to(a_w2[0], 0, lora).astype(BF16)
    a1 = _pad_to(a_a1[0], 1, lora).astype(BF16)
    a2 = _pad_to(a_a2[0], 0, lora).astype(BF16)
    g1 = _pad_to(a_g1[0], 1, 2 * lora).astype(BF16)
    g2 = _pad_to(a_g2[0], 0, 2 * lora).astype(BF16)
    vec = _pad_to(jnp.stack([a_w0[0], a_a0[0], a_k_k[0], a_k_a[0], a_r_k[0].reshape(d)]), 0, 8)
    gn = jnp.stack([a_gn_g[0], a_gn_b[0]])
    a_wo = a_w_o[0].astype(BF16)
    ffn1 = ffn_w1.astype(BF16)
    ffn2 = ffn_w2.astype(BF16)
    wk = kv_w_k.astype(BF16)
    wv = kv_w_v.astype(BF16)
    wq = b_w_q[0].astype(BF16)
    b_wo = b_w_o[0].astype(BF16)
    g_tiled = jnp.tile(b_subln_g[0], heads).reshape(1, d)
    g_head = b_subln_g[0].reshape(1, LANES)
    lam_init = 0.8 - 0.6 * math.exp(-0.3 * n_a)
    ln = [[jnp.stack([ln_g[l, s], ln_b[l, s]]) for s in range(2)] for l in range(depth)]

    def trunk(x, rows, shift0, wkv0, seq_len, tm, tm_mlp, tc, attend):
        b = x.shape[0]
        x2d = x.reshape(b * seq_len, d)
        tiles = max(seq_len // tm, 1)
        tiles_mlp = max(seq_len // tm_mlp, 1)
        if seq_len == 1:
            shape_mod = lambda m: m[rows].reshape(1, b, -1)
            sh0 = shift0.reshape(1, b, d)
        else:
            shape_mod = lambda m: m[rows].reshape(b, 1, -1)
            sh0 = shift0.reshape(b, 1, d)
        mod0, mod1, kvmod = shape_mod(mods[0]), shape_mod(mods[1]), shape_mod(kvmods)

        *seqs, g, bonus, last = _rwkv_proj(x2d, mod0, sh0, seq_len, tm, a_mu[0], wrkv, w1, w2, a1, a2, g1, g2, vec)
        seqs = [s.reshape(b, seq_len, d) for s in seqs]
        y, s_fin = _wkv(seqs, _pack_state(wkv0), tc, 2)
        x1 = _mix_out(True, alpha, [y.reshape(b * seq_len, d), g, bonus], x2d, mod0, tiles, tm, a_wo, gn, ln[0][0])
        x2 = _mlp(alpha, x1, mod0, tiles_mlp, tm_mlp, ffn1[0], ffn2[0], ln[0][1])
        q_dtype = F32 if seq_len == 1 else BF16
        k, v, kb, vb, q = _qkv(x2, kvmod, mod1, tiles, tm, wk, wv, wq, B_HEAD ** -0.5, q_dtype)
        on = attend(q, k, v, kb, vb)
        x3 = _mix_out(False, alpha, [on], x2, mod1, tiles, tm, b_wo, gn, ln[1][0])
        y_out = _mlp(alpha, x3, mod1, tiles_mlp, tm_mlp, ffn1[1], ffn2[1], ln[1][1])
        return (y_out.reshape(b, seq_len, d), k.reshape(b, seq_len, heads, 2, B_HEAD),
                v.reshape(b, seq_len, heads, 2 * B_HEAD), _unpack_state(s_fin)[None],
                last.reshape(1, b, d))

    def attend_prompt(q, k, v, kb, vb):
        shp = (bp, seq, d)
        on = _diff_attn_prompt(q.reshape(shp), kb.reshape(shp), vb.reshape(shp), b_lam_q[0], b_lam_k[0], g_head,
                               lam_init, min(512, seq))
        return on.reshape(bp * seq, d)

    def attend_sample(q, k, v, kb, vb):
        shp = (bd, 1, d)
        on = _diff_attn_decode(q.reshape(shp), k.reshape(shp), v.reshape(shp), cache_k.reshape(pool, PAGE_SIZE, d),
                               cache_v.reshape(pool, PAGE_SIZE, d), page_table, b_lam_q[0], b_lam_k[0], g_tiled,
                               lam_init, 4)
        return on.reshape(bd, d)

    zeros_shift = jnp.zeros((bp, d), F32)
    zeros_wkv = jnp.zeros((bp, d // A_HEAD, A_HEAD, A_HEAD), F32)
    y_p, k_p, v_p, wkv_p, shift_p = trunk(x_prompt, slice(0, bp), zeros_shift, zeros_wkv, seq,
                                          min(256, seq), min(512, seq), min(128, seq), attend_prompt)
    y_s, k_s, v_s, wkv_s, shift_s = trunk(x_sample, slice(bp, bp + bd), state_shift[0], state_wkv[0], 1,
                                          bd, bd, 1, attend_sample)
    return (y_p, y_s, k_p, v_p, wkv_p, shift_p, k_s, v_s, wkv_s, shift_s)
```

```python
import functools
import math

import jax
import jax.numpy as jnp
from jax import lax
from jax.experimental import pallas as pl
from jax.experimental.pallas import tpu as pltpu

F32 = jnp.float32
BF16 = jnp.bfloat16

LANES = 128
A_HEAD = 64
B_HEAD = 64
PAGE_SIZE = 128
GN_EPS = 64e-5
SUBLN_EPS = 1e-5
LN_EPS = 1e-5
NEG = -1e30
VMEM_LIMIT = 56 * 2**20


def _params(sem):
    return pltpu.CompilerParams(dimension_semantics=sem, vmem_limit_bytes=VMEM_LIMIT)


def _dot(a, b):
    return jnp.dot(a, b, preferred_element_type=F32)


def _dot_nt(a, b):
    return lax.dot_general(a, b, (((1,), (1,)), ((), ())), preferred_element_type=F32)


def _split(x):
    hi = x.astype(BF16)
    lo = (x - hi.astype(F32)).astype(BF16)
    return hi, lo


def _sigmoid(z):
    return 1.0 / (1.0 + jnp.exp(-z))


def _layer_norm(z, g, b):
    mu = jnp.mean(z, -1, keepdims=True)
    d = z - mu
    var = jnp.mean(d * d, -1, keepdims=True)
    return d * lax.rsqrt(var + LN_EPS) * g + b


def _pair_ones():
    r = lax.broadcasted_iota(jnp.int32, (LANES, LANES), 0) >> 6
    c = lax.broadcasted_iota(jnp.int32, (LANES, LANES), 1) >> 6
    return jnp.where(r == c, 1.0, 0.0).astype(BF16)


def _head_sum(x, ones):
    m, d = x.shape
    cols = d // LANES
    stacked = jnp.concatenate([x[:, c * LANES:(c + 1) * LANES] for c in range(cols)], axis=0)
    hi, lo = _split(stacked)
    s = _dot(hi, ones) + _dot(lo, ones)
    return jnp.concatenate([s[c * m:(c + 1) * m] for c in range(cols)], axis=1)


def _mod_spec(mod, tiles_per_batch):
    return pl.BlockSpec((1, mod.shape[1], mod.shape[2]), lambda i, *_: (i // tiles_per_batch, 0, 0))


def _full(a):
    nd = a.ndim
    return pl.BlockSpec(a.shape, lambda *_: (0,) * nd)


def _ada_kernel(c_ref, w_ref, b_ref, o_ref):
    c = c_ref[...]
    s = c * _sigmoid(c)
    o_ref[...] = _dot(s.astype(BF16), w_ref[...].astype(BF16)) + b_ref[...]


def _ada(c, w, b):
    m, d = c.shape
    n = w.shape[1]
    tn = 1024
    return pl.pallas_call(
        _ada_kernel,
        out_shape=jax.ShapeDtypeStruct((m, n), F32),
        grid=(n // tn,),
        in_specs=[pl.BlockSpec((m, d), lambda j: (0, 0)),
                  pl.BlockSpec((d, tn), lambda j: (0, j)),
                  pl.BlockSpec((1, tn), lambda j: (0, j))],
        out_specs=pl.BlockSpec((m, tn), lambda j: (0, j)),
        compiler_params=_params(("arbitrary",)),
        name="ada_proj",
    )(c, w, b.reshape(1, n))


def _rwkv_proj_kernel(single_step, tiles_per_batch, d,
                      x_ref, xp_ref, mod_ref, sh0_ref, mu_ref, wrkv_ref, w1_ref, w2_ref, a1_ref, a2_ref,
                      g1_ref, g2_ref, vec_ref,
                      r_o, w_o, k_o, v_o, a_o, b_o, g_o, bonus_o, last_o):
    mod = mod_ref[0]
    sh = mod[:, 0:d]
    sc = mod[:, d:2 * d]
    x = x_ref[...]
    h = x * (1.0 + sc) + sh
    if single_step:
        h_prev = sh0_ref[0]
        last_o[0] = h
    else:
        i = pl.program_id(0)
        prev_row = xp_ref[7:8, :] * (1.0 + sc) + sh
        first = jnp.where(i % tiles_per_batch == 0, sh0_ref[0], prev_row)
        rolled = pltpu.roll(h, 1, 0)
        rid = lax.broadcasted_iota(jnp.int32, h.shape, 0)
        h_prev = jnp.where(rid == 0, first, rolled)
        last_o[0] = h[h.shape[0] - 1:, :]
    xx = h_prev - h

    def mix(p):
        return (h + xx * mu_ref[p:p + 1, :]).astype(BF16)

    r = _dot(mix(0), wrkv_ref[0])
    k = _dot(mix(1), wrkv_ref[1])
    v = _dot(mix(2), wrkv_ref[2])
    lw = _dot(jnp.tanh(_dot(mix(3), w1_ref[...])).astype(BF16), w2_ref[...])
    la = _dot(_dot(mix(4), a1_ref[...]).astype(BF16), a2_ref[...])
    g = _dot(_sigmoid(_dot(mix(5), g1_ref[...])).astype(BF16), g2_ref[...])

    w0 = vec_ref[0:1, :]
    a0 = vec_ref[1:2, :]
    k_k = vec_ref[2:3, :]
    k_a = vec_ref[3:4, :]
    r_k = vec_ref[4:5, :]

    z = -(w0 + lw)
    softplus = jnp.maximum(z, 0.0) + jnp.log(1.0 + jnp.exp(-jnp.abs(z)))
    w_log = -softplus - 0.5
    decay = jnp.exp(-jnp.exp(w_log))
    a_gate = _sigmoid(a0 + la)
    ones = _pair_ones()
    kkf = k * k_k
    norm = jnp.sqrt(_head_sum(kkf * kkf, ones))
    kk = kkf / jnp.maximum(norm, 1e-12)
    k_mod = k * (1.0 + (a_gate - 1.0) * k_a)
    bonus = _head_sum(r * k_mod * r_k, ones) * v

    r_o[...] = r
    w_o[...] = decay
    k_o[...] = k_mod
    v_o[...] = v
    a_o[...] = -kk
    b_o[...] = kk * a_gate
    g_o[...] = g
    bonus_o[...] = bonus


def _rwkv_proj(x2d, mod, shift0, seq_len, tm, mu, wrkv, w1, w2, a1, a2, g1, g2, vec):
    n, d = x2d.shape
    single = seq_len == 1
    tiles_per_batch = 1 if single else seq_len // tm
    nt = n // tm
    nb = mod.shape[0]
    last_rows = tm if single else 1
    row = pl.BlockSpec((tm, d), lambda i: (i, 0))
    if single:
        xp_spec = pl.BlockSpec((8, d), lambda i: (0, 0))
        sh0_spec = pl.BlockSpec((1, tm, d), lambda i: (i, 0, 0))
    else:
        xp_spec = pl.BlockSpec((8, d), lambda i: (jnp.maximum(i * (tm // 8) - 1, 0), 0))
        sh0_spec = pl.BlockSpec((1, 1, d), lambda i: (i // tiles_per_batch, 0, 0))
    out_sds = jax.ShapeDtypeStruct((n, d), F32)
    outs = pl.pallas_call(
        functools.partial(_rwkv_proj_kernel, single, tiles_per_batch, d),
        out_shape=[out_sds] * 8 + [jax.ShapeDtypeStruct((nb, last_rows, d), F32)],
        grid=(nt,),
        in_specs=[row, xp_spec, _mod_spec(mod, tiles_per_batch), sh0_spec, _full(mu), _full(wrkv), _full(w1),
                  _full(w2), _full(a1), _full(a2), _full(g1), _full(g2), _full(vec)],
        out_specs=[row] * 8 + [pl.BlockSpec((1, last_rows, d), lambda i: (i // tiles_per_batch, 0, 0))],
        compiler_params=_params(("arbitrary",)),
        name="rwkv_proj",
    )(x2d, x2d, mod, shift0, mu, wrkv, w1, w2, a1, a2, g1, g2, vec)
    return outs


def _wkv_kernel(tc, r_ref, w_ref, k_ref, v_ref, a_ref, b_ref, s0_ref, y_ref, s_ref):
    @pl.when(pl.program_id(1) == 0)
    def _():
        s_ref[...] = s0_ref[...]

    groups, pairs = s_ref.shape[0], s_ref.shape[1]
    ones = _pair_ones()
    rid = lax.broadcasted_iota(jnp.int32, (A_HEAD, LANES), 0)
    cid = lax.broadcasted_iota(jnp.int32, (A_HEAD, LANES), 1) & (A_HEAD - 1)
    diag = rid == cid
    idx = [(g, p) for g in range(groups) for p in range(pairs)]
    n = len(idx)

    rb = min(8, tc)
    sub = lax.broadcasted_iota(jnp.int32, (rb, LANES), 0)

    def block(tb, carry):
        base = pl.multiple_of(tb * rb, rb)

        def tile(ref, g, p):
            return ref[g, pl.ds(base, rb), p * LANES:(p + 1) * LANES]

        r_t, w_t, k_t, v_t, a_t, b_t = ([tile(ref, g, p) for g, p in idx]
                                        for ref in (r_ref, w_ref, k_ref, v_ref, a_ref, b_ref))
        states = [s_ref[g, p] for g, p in idx]
        y_tiles = [jnp.zeros((rb, LANES), F32) for _ in idx]
        for i in range(rb):
            row = lambda tiles, j: tiles[j][i:i + 1, :]
            sa_in = jnp.concatenate([(states[j] * row(a_t, j)).astype(BF16) for j in range(n)], axis=0)
            vd = jnp.concatenate([jnp.where(diag, row(v_t, j), 0.0).astype(BF16) for j in range(n)], axis=0)
            sa = _dot(sa_in, ones)
            vb = _dot(vd, ones)
            for j in range(n):
                sl = slice(j * A_HEAD, (j + 1) * A_HEAD)
                states[j] = states[j] * row(w_t, j) + sa[sl] * row(b_t, j) + vb[sl] * row(k_t, j)
            y_in = jnp.concatenate([(states[j] * row(r_t, j)).astype(BF16) for j in range(n)], axis=0)
            yb = _dot(y_in, ones)
            for j in range(n):
                sl = slice(j * A_HEAD, (j + 1) * A_HEAD)
                y_row = jnp.sum(jnp.where(diag, yb[sl], 0.0), axis=0, keepdims=True)
                y_tiles[j] = jnp.where(sub == i, y_row, y_tiles[j])
        for j, (g, p) in enumerate(idx):
            s_ref[g, p] = states[j]
            y_ref[g, pl.ds(base, rb), p * LANES:(p + 1) * LANES] = y_tiles[j]
        return carry

    lax.fori_loop(0, tc // rb, block, 0)


def _wkv(seqs, s0, tc, group):
    b, t, d = seqs[0].shape
    pairs = s0.shape[1]
    seq_spec = pl.BlockSpec((group, tc, d), lambda i, j: (i, j, 0))
    st_spec = pl.BlockSpec((group, pairs, A_HEAD, LANES), lambda i, j: (i, 0, 0, 0))
    return pl.pallas_call(
        functools.partial(_wkv_kernel, tc),
        out_shape=[jax.ShapeDtypeStruct((b, t, d), F32), jax.ShapeDtypeStruct(s0.shape, F32)],
        grid=(b // group, t // tc),
        in_specs=[seq_spec] * 6 + [st_spec],
        out_specs=[seq_spec, st_spec],
        compiler_params=_params(("arbitrary", "arbitrary")),
        name="wkv7_scan",
    )(*seqs, s0)


def _mix_out_kernel(rwkv, alpha, d, *refs):
    if rwkv:
        y_ref, g_ref, bonus_ref, x_ref, mod_ref, wo_ref, gn_ref, ln_ref, o_ref = refs
        y = y_ref[...]
        ones = _pair_ones()
        mu_y = _head_sum(y, ones) * (1.0 / A_HEAD)
        dy = y - mu_y
        var_y = _head_sum(dy * dy, ones) * (1.0 / A_HEAD)
        yn = dy * lax.rsqrt(var_y + GN_EPS) * gn_ref[0:1, :] + gn_ref[1:2, :]
        pre = ((yn + bonus_ref[...]) * g_ref[...]).astype(BF16)
    else:
        on_ref, x_ref, mod_ref, wo_ref, ln_ref, o_ref = refs
        pre = on_ref[...].astype(BF16)
    mix = _dot(pre, wo_ref[...])
    gt = mod_ref[0][:, 2 * d:3 * d]
    z = alpha * x_ref[...] + (1.0 + gt) * mix
    o_ref[...] = _layer_norm(z, ln_ref[0:1, :], ln_ref[1:2, :])


def _mix_out(rwkv, alpha, ins, x2d, mod, tiles_per_batch, tm, wo, gn, ln):
    n, d = x2d.shape
    row = pl.BlockSpec((tm, d), lambda i: (i, 0))
    extra = [gn] if rwkv else []
    args = list(ins) + [x2d, mod, wo] + extra + [ln]
    specs = [row] * (len(ins) + 1) + [_mod_spec(mod, tiles_per_batch), _full(wo)] + [_full(e) for e in extra] + [_full(ln)]
    return pl.pallas_call(
        functools.partial(_mix_out_kernel, rwkv, alpha, d),
        out_shape=jax.ShapeDtypeStruct((n, d), F32),
        grid=(n // tm,),
        in_specs=specs,
        out_specs=row,
        compiler_params=_params(("arbitrary",)),
        name="rwkv_out_ln" if rwkv else "attn_out_ln",
    )(*args)


def _mlp_kernel(alpha, d, x_ref, mod_ref, w1_ref, w2_ref, ln_ref, o_ref, hf_ref, acc_ref):
    j = pl.program_id(1)
    mod = mod_ref[0]

    @pl.when(j == 0)
    def _():
        hf_ref[...] = (x_ref[...] * (1.0 + mod[:, 4 * d:5 * d]) + mod[:, 3 * d:4 * d]).astype(BF16)
        acc_ref[...] = jnp.zeros_like(acc_ref)

    h1 = jnp.maximum(_dot(hf_ref[...], w1_ref[...]), 0.0)
    acc_ref[...] += _dot((h1 * h1).astype(BF16), w2_ref[...])

    @pl.when(j == pl.num_programs(1) - 1)
    def _():
        z = alpha * x_ref[...] + (1.0 + mod[:, 5 * d:6 * d]) * acc_ref[...]
        o_ref[...] = _layer_norm(z, ln_ref[0:1, :], ln_ref[1:2, :])


def _mlp(alpha, x2d, mod, tiles_per_batch, tm, w1, w2, ln):
    n, d = x2d.shape
    dff = w1.shape[1]
    tf = 512
    row = pl.BlockSpec((tm, d), lambda i, j: (i, 0))
    return pl.pallas_call(
        functools.partial(_mlp_kernel, alpha, d),
        out_shape=jax.ShapeDtypeStruct((n, d), F32),
        grid=(n // tm, dff // tf),
        in_specs=[row, _mod_spec(mod, tiles_per_batch),
                  pl.BlockSpec((d, tf), lambda i, j: (0, j)),
                  pl.BlockSpec((tf, d), lambda i, j: (j, 0)),
                  _full(ln)],
        out_specs=row,
        scratch_shapes=[pltpu.VMEM((tm, d), BF16), pltpu.VMEM((tm, d), F32)],
        compiler_params=_params(("arbitrary", "arbitrary")),
        name="sq_relu_mlp_ln",
    )(x2d, mod, w1, w2, ln)


def _qkv_kernel(d, scale, transposed, x_ref, kvmod_ref, mod_ref, wk_ref, wv_ref, wq_ref, k_o, v_o, *extra):
    x = x_ref[...]
    kvmod = kvmod_ref[0]
    hkv = (x * (1.0 + kvmod[:, d:2 * d]) + kvmod[:, 0:d]).astype(BF16)
    k = _dot(hkv, wk_ref[...])
    v = _dot(hkv, wv_ref[...])
    k_o[...] = k
    v_o[...] = v
    mod = mod_ref[0]
    h = (x * (1.0 + mod[:, d:2 * d]) + mod[:, 0:d]).astype(BF16)
    q = _dot(h, wq_ref[...]) * scale
    if transposed:
        kb_o, vt_o, qt_o = extra
        kb_o[...] = k.astype(BF16)
        vt_o[0] = v.T.astype(BF16)
        qt_o[0] = q.T.astype(BF16)
    else:
        (q_o,) = extra
        q_o[...] = q


def _qkv(x2d, kvmod, mod, tiles_per_batch, tm, wk, wv, wq, scale, transposed):
    n, d = x2d.shape
    row = pl.BlockSpec((tm, d), lambda i: (i, 0))
    out_shape = [jax.ShapeDtypeStruct((n, d), F32)] * 2
    out_specs = [row] * 2
    if transposed:
        nb = n // (tiles_per_batch * tm)
        col = pl.BlockSpec((1, d, tm), lambda i: (i // tiles_per_batch, 0, i % tiles_per_batch))
        t_sds = jax.ShapeDtypeStruct((nb, d, tiles_per_batch * tm), BF16)
        out_shape += [jax.ShapeDtypeStruct((n, d), BF16), t_sds, t_sds]
        out_specs += [row, col, col]
    else:
        out_shape += [jax.ShapeDtypeStruct((n, d), F32)]
        out_specs += [row]
    return pl.pallas_call(
        functools.partial(_qkv_kernel, d, scale, transposed),
        out_shape=out_shape,
        grid=(n // tm,),
        in_specs=[row, _mod_spec(kvmod, tiles_per_batch), _mod_spec(mod, tiles_per_batch), _full(wk), _full(wv), _full(wq)],
        out_specs=out_specs,
        compiler_params=_params(("arbitrary",)),
        name="qkv_proj",
    )(x2d, kvmod, mod, wk, wv, wq)


def _diff_lambda(lq_ref, lk_ref, lam_init):
    lq = lq_ref[...]
    lk = lk_ref[...]
    dots = jnp.sum(lq * lk, -1, keepdims=True)
    return jnp.exp(dots[0:1, :]) - jnp.exp(dots[1:2, :]) + lam_init


def _attn_kernel(tq, lam_init, qi_tab, ki_tab, qt_ref, k_ref, vt_ref, lq_ref, lk_ref, g_ref, o_ref,
                 qs_ref, m_ref, l_ref, acc_ref):
    t = pl.program_id(2)
    qi = qi_tab[t]
    ki = ki_tab[t]

    @pl.when(ki == 0)
    def _():
        qt = qt_ref[0]
        row = lax.broadcasted_iota(jnp.int32, qt.shape, 0)
        qs_ref[:, 0:tq] = jnp.where(row < B_HEAD, qt, jnp.zeros_like(qt))
        qs_ref[:, tq:2 * tq] = jnp.where(row >= B_HEAD, qt, jnp.zeros_like(qt))
        m_ref[...] = jnp.full(m_ref.shape, NEG, F32)
        l_ref[...] = jnp.zeros_like(l_ref)
        acc_ref[...] = jnp.zeros_like(acc_ref)

    def update(diagonal):
        s = _dot(k_ref[0], qs_ref[...])
        if diagonal:
            kpos = lax.broadcasted_iota(jnp.int32, s.shape, 0)
            qpos = lax.broadcasted_iota(jnp.int32, s.shape, 1) & (tq - 1)
            s = jnp.where(kpos <= qpos, s, NEG)
        m_prev = m_ref[...]
        m_new = jnp.maximum(m_prev, jnp.max(s, 0, keepdims=True))
        alpha = jnp.exp(m_prev - m_new)
        p = jnp.exp(s - m_new)
        l_ref[...] = alpha * l_ref[...] + jnp.sum(p, 0, keepdims=True)
        acc_ref[...] = alpha * acc_ref[...] + _dot(vt_ref[0], p.astype(BF16))
        m_ref[...] = m_new

    @pl.when(ki < qi)
    def _():
        update(False)

    @pl.when(ki == qi)
    def _():
        update(True)
        o_all = acc_ref[...] * (1.0 / l_ref[...])
        lam = _diff_lambda(lq_ref, lk_ref, lam_init)
        o = o_all[:, 0:tq] - lam * o_all[:, tq:2 * tq]
        on = o * lax.rsqrt(jnp.mean(o * o, 0, keepdims=True) + SUBLN_EPS) * g_ref[...] * (1.0 - lam_init)
        o_ref[0] = on.T.astype(o_ref.dtype)


def _diff_attn_prompt(qt, k, vt, lam_q, lam_k, g_col, lam_init, tq):
    b, s, d = k.shape
    heads = d // LANES
    nq = s // tq
    pairs = [(qi, ki) for qi in range(nq) for ki in range(qi + 1)]
    qi_tab = jnp.asarray([p[0] for p in pairs], jnp.int32)
    ki_tab = jnp.asarray([p[1] for p in pairs], jnp.int32)
    qt_spec = pl.BlockSpec((1, LANES, tq), lambda bi, h, t, qt_, kt_: (bi, h, qt_[t]))
    vt_spec = pl.BlockSpec((1, LANES, tq), lambda bi, h, t, qt_, kt_: (bi, h, kt_[t]))
    k_spec = pl.BlockSpec((1, tq, LANES), lambda bi, h, t, qt_, kt_: (bi, kt_[t], h))
    o_spec = pl.BlockSpec((1, tq, LANES), lambda bi, h, t, qt_, kt_: (bi, qt_[t], h))
    small = lambda a: pl.BlockSpec(a.shape, lambda *_: (0,) * a.ndim)
    grid_spec = pltpu.PrefetchScalarGridSpec(
        num_scalar_prefetch=2,
        grid=(b, heads, len(pairs)),
        in_specs=[qt_spec, k_spec, vt_spec, small(lam_q), small(lam_k), small(g_col)],
        out_specs=o_spec,
        scratch_shapes=[pltpu.VMEM((LANES, 2 * tq), BF16), pltpu.VMEM((1, 2 * tq), F32),
                        pltpu.VMEM((1, 2 * tq), F32), pltpu.VMEM((LANES, 2 * tq), F32)],
    )
    return pl.pallas_call(
        functools.partial(_attn_kernel, tq, lam_init),
        out_shape=jax.ShapeDtypeStruct((b, s, d), BF16),
        grid_spec=grid_spec,
        compiler_params=_params(("arbitrary",) * 3),
        name="diff_attn_prompt",
    )(qi_tab, ki_tab, qt, k, vt, lam_q, lam_k, g_col)


def _decode_kernel(pp, lam_init, pt_ref, q_ref, kn_ref, vn_ref, lq_ref, lk_ref, g_ref, *rest):
    k_refs = rest[:pp]
    v_refs = rest[pp:2 * pp]
    o_ref, m_ref, l_ref, acc_ref = rest[2 * pp:]
    step = pl.program_id(1)

    @pl.when(step == 0)
    def _():
        m_ref[...] = jnp.full(m_ref.shape, NEG, F32)
        l_ref[...] = jnp.zeros_like(l_ref)
        acc_ref[...] = jnp.zeros_like(acc_ref)

    r = lax.broadcasted_iota(jnp.int32, (LANES, LANES), 0)
    map_ones = [jnp.where(r < B_HEAD, 1.0, 0.0).astype(BF16), jnp.where(r >= B_HEAD, 1.0, 0.0).astype(BF16)]
    q = q_ref[0]
    heads = q.shape[0]

    def scores(k3):
        n = k3.shape[0]
        kq = (k3 * q).reshape(n * heads, LANES).astype(BF16)
        return [_dot(kq, e).reshape(n, heads, LANES) for e in map_ones]

    for i in range(pp):
        v3 = v_refs[i][0]
        for mi, s in enumerate(scores(k_refs[i][0])):
            m_prev = m_ref[mi]
            m_new = jnp.maximum(m_prev, jnp.max(s, 0))
            alpha = jnp.exp(m_prev - m_new)
            p = jnp.exp(s - m_new)
            l_ref[mi] = alpha * l_ref[mi] + jnp.sum(p, 0)
            acc_ref[mi] = alpha * acc_ref[mi] + jnp.sum(p * v3, 0)
            m_ref[mi] = m_new

    @pl.when(step == pl.num_programs(1) - 1)
    def _():
        outs = []
        for mi, s in enumerate(scores(kn_ref[...])):
            m_prev = m_ref[mi]
            m_fin = jnp.maximum(m_prev, s[0])
            alpha = jnp.exp(m_prev - m_fin)
            p_new = jnp.exp(s[0] - m_fin)
            l_fin = alpha * l_ref[mi] + p_new
            outs.append((alpha * acc_ref[mi] + p_new * vn_ref[0]) / l_fin)
        lam = _diff_lambda(lq_ref, lk_ref, lam_init)
        o = outs[0] - lam * outs[1]
        on = o * lax.rsqrt(jnp.mean(o * o, -1, keepdims=True) + SUBLN_EPS) * g_ref[...] * (1.0 - lam_init)
        o_ref[0] = on


def _diff_attn_decode(q, k_new, v_new, cache_k, cache_v, page_table, lam_q, lam_k, g_row, lam_init, pp):
    bd, heads, _ = q.shape
    n_pages = page_table.shape[1]
    row = pl.BlockSpec((1, heads, LANES), lambda b, s, pt: (b, 0, 0))

    def page_spec(i):
        return pl.BlockSpec((1, PAGE_SIZE, heads, LANES), lambda b, s, pt: (pt[b, s * pp + i], 0, 0, 0))

    small = lambda a: pl.BlockSpec(a.shape, lambda b, s, pt: (0,) * a.ndim)
    grid_spec = pltpu.PrefetchScalarGridSpec(
        num_scalar_prefetch=1,
        grid=(bd, n_pages // pp),
        in_specs=[row, row, row, small(lam_q), small(lam_k), small(g_row)]
                 + [page_spec(i) for i in range(pp)] * 2,
        out_specs=row,
        scratch_shapes=[pltpu.VMEM((2, heads, LANES), F32)] * 3,
    )
    return pl.pallas_call(
        functools.partial(_decode_kernel, pp, lam_init),
        out_shape=jax.ShapeDtypeStruct((bd, heads, LANES), F32),
        grid_spec=grid_spec,
        compiler_params=_params(("arbitrary", "arbitrary")),
        name="diff_attn_decode",
    )(page_table, q, k_new, v_new, lam_q, lam_k, g_row, *([cache_k] * pp), *([cache_v] * pp))


def _pad_to(a, axis, size):
    pad = [(0, 0)] * a.ndim
    pad[axis] = (0, size - a.shape[axis])
    return jnp.pad(a, pad)


def _pack_state(s):
    b, h, n, _ = s.shape
    return s.reshape(b, h // 2, 2, n, n).transpose(0, 1, 3, 2, 4).reshape(b, h // 2, n, 2 * n)


def _unpack_state(s):
    b, p, n, _ = s.shape
    return s.reshape(b, p, n, 2, n).transpose(0, 1, 3, 2, 4).reshape(b, 2 * p, n, n)


def kernel(x_prompt, x_sample, cache_k, cache_v, state_wkv, state_shift, page_table, c_prompt, c_sample, ln_g, ln_b, ada_w, ada_b, ffn_w1, ffn_w2, a_mu, a_w_rkv, a_w_o, a_w0, a_w1, a_w2, a_a0, a_a1, a_a2, a_g1, a_g2, a_k_k, a_k_a, a_r_k, a_gn_g, a_gn_b, kv_ada_w, kv_ada_b, kv_w_k, kv_w_v, b_w_q, b_w_o, b_lam_q, b_lam_k, b_subln_g):
    bp, seq, d = x_prompt.shape
    bd, dec_seq, _ = x_sample.shape
    depth = ada_w.shape[0]
    n_a = a_mu.shape[0]
    assert depth == 2 and n_a == 1 and dec_seq == 1
    alpha = (2 * depth) ** 0.25
    heads = d // LANES
    pool = cache_k.shape[0]

    n_c = bp + bd
    c_all = _pad_to(jnp.concatenate([c_prompt, c_sample], axis=0), 0, -(-n_c // 8) * 8)
    mods = [_ada(c_all, ada_w[l], ada_b[l]) for l in range(depth)]
    kvmods = _ada(c_all, kv_ada_w, kv_ada_b)

    wrkv = a_w_rkv[0].astype(BF16)
    lora = 128
    w1 = _pad_to(a_w1[0], 1, lora).astype(BF16)
    w2 = _pad_to(a_w2[0], 0, lora).astype(BF16)
    a1 = _pad_to(a_a1[0], 1, lora).astype(BF16)
    a2 = _pad_to(a_a2[0], 0, lora).astype(BF16)
    g1 = _pad_to(a_g1[0], 1, 2 * lora).astype(BF16)
    g2 = _pad_to(a_g2[0], 0, 2 * lora).astype(BF16)
    vec = _pad_to(jnp.stack([a_w0[0], a_a0[0], a_k_k[0], a_k_a[0], a_r_k[0].reshape(d)]), 0, 8)
    gn = jnp.stack([a_gn_g[0], a_gn_b[0]])
    a_wo = a_w_o[0].astype(BF16)
    ffn1 = ffn_w1.astype(BF16)
    ffn2 = ffn_w2.astype(BF16)
    wk = kv_w_k.astype(BF16)
    wv = kv_w_v.astype(BF16)
    wq = b_w_q[0].astype(BF16)
    b_wo = b_w_o[0].astype(BF16)
    g_head = b_subln_g[0].reshape(1, LANES)
    lam_init = 0.8 - 0.6 * math.exp(-0.3 * n_a)
    ln = [[jnp.stack([ln_g[l, s], ln_b[l, s]]) for s in range(2)] for l in range(depth)]

    def trunk(x, rows, shift0, wkv0, seq_len, tm, tm_mlp, tc, attend):
        b = x.shape[0]
        x2d = x.reshape(b * seq_len, d)
        tiles = max(seq_len // tm, 1)
        tiles_mlp = max(seq_len // tm_mlp, 1)
        if seq_len == 1:
            shape_mod = lambda m: m[rows].reshape(1, b, -1)
            sh0 = shift0.reshape(1, b, d)
        else:
            shape_mod = lambda m: m[rows].reshape(b, 1, -1)
            sh0 = shift0.reshape(b, 1, d)
        mod0, mod1, kvmod = shape_mod(mods[0]), shape_mod(mods[1]), shape_mod(kvmods)

        *seqs, g, bonus, last = _rwkv_proj(x2d, mod0, sh0, seq_len, tm, a_mu[0], wrkv, w1, w2, a1, a2, g1, g2, vec)
        seqs = [s.reshape(b, seq_len, d) for s in seqs]
        y, s_fin = _wkv(seqs, _pack_state(wkv0), tc, 2)
        x1 = _mix_out(True, alpha, [y.reshape(b * seq_len, d), g, bonus], x2d, mod0, tiles, tm, a_wo, gn, ln[0][0])
        x2 = _mlp(alpha, x1, mod0, tiles_mlp, tm_mlp, ffn1[0], ffn2[0], ln[0][1])
        k, v, *attn_in = _qkv(x2, kvmod, mod1, tiles, tm, wk, wv, wq, B_HEAD ** -0.5, seq_len > 1)
        on = attend(k, v, *attn_in)
        x3 = _mix_out(False, alpha, [on], x2, mod1, tiles, tm, b_wo, gn, ln[1][0])
        y_out = _mlp(alpha, x3, mod1, tiles_mlp, tm_mlp, ffn1[1], ffn2[1], ln[1][1])
        return (y_out.reshape(b, seq_len, d), k.reshape(b, seq_len, heads, 2, B_HEAD),
                v.reshape(b, seq_len, heads, 2 * B_HEAD), _unpack_state(s_fin)[None],
                last.reshape(1, b, d))

    def attend_prompt(k, v, kb, vt, qt):
        on = _diff_attn_prompt(qt, kb.reshape(bp, seq, d), vt, b_lam_q[0], b_lam_k[0], g_head.reshape(LANES, 1),
                               lam_init, min(1024, seq))
        return on.reshape(bp * seq, d)

    def attend_sample(k, v, q):
        shp = (bd, heads, LANES)
        on = _diff_attn_decode(q.reshape(shp), k.reshape(shp), v.reshape(shp),
                               cache_k.reshape(pool, PAGE_SIZE, heads, LANES), cache_v, page_table,
                               b_lam_q[0], b_lam_k[0], g_head, lam_init, min(8, page_table.shape[1]))
        return on.reshape(bd, d)

    zeros_shift = jnp.zeros((bp, d), F32)
    zeros_wkv = jnp.zeros((bp, d // A_HEAD, A_HEAD, A_HEAD), F32)
    y_p, k_p, v_p, wkv_p, shift_p = trunk(x_prompt, slice(0, bp), zeros_shift, zeros_wkv, seq,
                                          min(256, seq), min(512, seq), min(128, seq), attend_prompt)
    y_s, k_s, v_s, wkv_s, shift_s = trunk(x_sample, slice(bp, bp + bd), state_shift[0], state_wkv[0], 1,
                                          bd, bd, 1, attend_sample)
    return (y_p, y_s, k_p, v_p, wkv_p, shift_p, k_s, v_s, wkv_s, shift_s)
```

```python
import functools
import math

import jax
import jax.numpy as jnp
from jax import lax
from jax.experimental import pallas as pl
from jax.experimental.pallas import tpu as pltpu

F32 = jnp.float32
BF16 = jnp.bfloat16

LANES = 128
A_HEAD = 64
B_HEAD = 64
PAGE_SIZE = 128
GN_EPS = 64e-5
SUBLN_EPS = 1e-5
LN_EPS = 1e-5
NEG = -1e30
LOG2E = math.log2(math.e)
VMEM_LIMIT = 56 * 2**20


def _params(sem):
    return pltpu.CompilerParams(dimension_semantics=sem, vmem_limit_bytes=VMEM_LIMIT)


def _dot(a, b):
    return jnp.dot(a, b, preferred_element_type=F32)


def _split(x):
    hi = x.astype(BF16)
    lo = (x - hi.astype(F32)).astype(BF16)
    return hi, lo


def _sigmoid(z):
    return 1.0 / (1.0 + jnp.exp(-z))


def _layer_norm(z, g, b):
    mu = jnp.mean(z, -1, keepdims=True)
    d = z - mu
    var = jnp.mean(d * d, -1, keepdims=True)
    return d * lax.rsqrt(var + LN_EPS) * g + b


def _pair_ones():
    r = lax.broadcasted_iota(jnp.int32, (LANES, LANES), 0) >> 6
    c = lax.broadcasted_iota(jnp.int32, (LANES, LANES), 1) >> 6
    return jnp.where(r == c, 1.0, 0.0).astype(BF16)


def _head_sum(x, ones):
    m, d = x.shape
    cols = d // LANES
    stacked = jnp.concatenate([x[:, c * LANES:(c + 1) * LANES] for c in range(cols)], axis=0)
    hi, lo = _split(stacked)
    s = _dot(hi, ones) + _dot(lo, ones)
    return jnp.concatenate([s[c * m:(c + 1) * m] for c in range(cols)], axis=1)


def _mod_spec(mod, tiles_per_batch):
    return pl.BlockSpec((1, mod.shape[1], mod.shape[2]), lambda i, *_: (i // tiles_per_batch, 0, 0))


def _full(a):
    nd = a.ndim
    return pl.BlockSpec(a.shape, lambda *_: (0,) * nd)


def _ada_kernel(c_ref, w_ref, b_ref, o_ref):
    c = c_ref[...]
    s = c * _sigmoid(c)
    o_ref[...] = _dot(s.astype(BF16), w_ref[...].astype(BF16)) + b_ref[...]


def _ada(c, w, b):
    m, d = c.shape
    n = w.shape[1]
    tn = 1024
    return pl.pallas_call(
        _ada_kernel,
        out_shape=jax.ShapeDtypeStruct((m, n), F32),
        grid=(n // tn,),
        in_specs=[pl.BlockSpec((m, d), lambda j: (0, 0)),
                  pl.BlockSpec((d, tn), lambda j: (0, j)),
                  pl.BlockSpec((1, tn), lambda j: (0, j))],
        out_specs=pl.BlockSpec((m, tn), lambda j: (0, j)),
        compiler_params=_params(("arbitrary",)),
        name="ada_proj",
    )(c, w, b.reshape(1, n))


def _rwkv_proj_kernel(single_step, tiles_per_batch, d,
                      x_ref, xp_ref, mod_ref, sh0_ref, mu_ref, wrkv_ref, w1_ref, w2_ref, a1_ref, a2_ref,
                      g1_ref, g2_ref, vec_ref,
                      r_o, w_o, k_o, v_o, a_o, b_o, g_o, bonus_o, last_o):
    mod = mod_ref[0]
    sh = mod[:, 0:d]
    sc = mod[:, d:2 * d]
    x = x_ref[...]
    h = x * (1.0 + sc) + sh
    if single_step:
        h_prev = sh0_ref[0]
        last_o[0] = h
    else:
        i = pl.program_id(0)
        prev_row = xp_ref[7:8, :] * (1.0 + sc) + sh
        first = jnp.where(i % tiles_per_batch == 0, sh0_ref[0], prev_row)
        rolled = pltpu.roll(h, 1, 0)
        rid = lax.broadcasted_iota(jnp.int32, h.shape, 0)
        h_prev = jnp.where(rid == 0, first, rolled)
        last_o[0] = h[h.shape[0] - 1:, :]
    xx = h_prev - h

    def mix(p):
        return (h + xx * mu_ref[p:p + 1, :]).astype(BF16)

    r = _dot(mix(0), wrkv_ref[0])
    k = _dot(mix(1), wrkv_ref[1])
    v = _dot(mix(2), wrkv_ref[2])
    lw = _dot(jnp.tanh(_dot(mix(3), w1_ref[...])).astype(BF16), w2_ref[...])
    la = _dot(_dot(mix(4), a1_ref[...]).astype(BF16), a2_ref[...])
    g = _dot(_sigmoid(_dot(mix(5), g1_ref[...])).astype(BF16), g2_ref[...])

    w0 = vec_ref[0:1, :]
    a0 = vec_ref[1:2, :]
    k_k = vec_ref[2:3, :]
    k_a = vec_ref[3:4, :]
    r_k = vec_ref[4:5, :]

    z = -(w0 + lw)
    softplus = jnp.maximum(z, 0.0) + jnp.log(1.0 + jnp.exp(-jnp.abs(z)))
    w_log = -softplus - 0.5
    decay = jnp.exp(-jnp.exp(w_log))
    a_gate = _sigmoid(a0 + la)
    ones = _pair_ones()
    kkf = k * k_k
    norm = jnp.sqrt(_head_sum(kkf * kkf, ones))
    kk = kkf / jnp.maximum(norm, 1e-12)
    k_mod = k * (1.0 + (a_gate - 1.0) * k_a)
    bonus = _head_sum(r * k_mod * r_k, ones) * v

    r_o[...] = r
    w_o[...] = decay
    k_o[...] = k_mod
    v_o[...] = v
    a_o[...] = -kk
    b_o[...] = kk * a_gate
    g_o[...] = g
    bonus_o[...] = bonus


def _rwkv_proj(x2d, mod, shift0, seq_len, tm, mu, wrkv, w1, w2, a1, a2, g1, g2, vec):
    n, d = x2d.shape
    single = seq_len == 1
    tiles_per_batch = 1 if single else seq_len // tm
    nt = n // tm
    nb = mod.shape[0]
    last_rows = tm if single else 1
    row = pl.BlockSpec((tm, d), lambda i: (i, 0))
    if single:
        xp_spec = pl.BlockSpec((8, d), lambda i: (0, 0))
        sh0_spec = pl.BlockSpec((1, tm, d), lambda i: (i, 0, 0))
    else:
        xp_spec = pl.BlockSpec((8, d), lambda i: (jnp.maximum(i * (tm // 8) - 1, 0), 0))
        sh0_spec = pl.BlockSpec((1, 1, d), lambda i: (i // tiles_per_batch, 0, 0))
    out_sds = jax.ShapeDtypeStruct((n, d), F32)
    outs = pl.pallas_call(
        functools.partial(_rwkv_proj_kernel, single, tiles_per_batch, d),
        out_shape=[out_sds] * 8 + [jax.ShapeDtypeStruct((nb, last_rows, d), F32)],
        grid=(nt,),
        in_specs=[row, xp_spec, _mod_spec(mod, tiles_per_batch), sh0_spec, _full(mu), _full(wrkv), _full(w1),
                  _full(w2), _full(a1), _full(a2), _full(g1), _full(g2), _full(vec)],
        out_specs=[row] * 8 + [pl.BlockSpec((1, last_rows, d), lambda i: (i // tiles_per_batch, 0, 0))],
        compiler_params=_params(("arbitrary",)),
        name="rwkv_proj",
    )(x2d, x2d, mod, shift0, mu, wrkv, w1, w2, a1, a2, g1, g2, vec)
    return outs


def _wkv_kernel(tc, r_ref, w_ref, k_ref, v_ref, a_ref, b_ref, s0_ref, y_ref, s_ref):
    @pl.when(pl.program_id(1) == 0)
    def _():
        s_ref[...] = s0_ref[...]

    groups, pairs = s_ref.shape[0], s_ref.shape[1]
    ones = _pair_ones()
    rid = lax.broadcasted_iota(jnp.int32, (A_HEAD, LANES), 0)
    cid = lax.broadcasted_iota(jnp.int32, (A_HEAD, LANES), 1) & (A_HEAD - 1)
    diag = rid == cid
    idx = [(g, p) for g in range(groups) for p in range(pairs)]
    n = len(idx)
    sets = [list(range(n))]

    rb = min(8, tc)
    sub = lax.broadcasted_iota(jnp.int32, (rb, LANES), 0)

    def block(tb, carry):
        base = pl.multiple_of(tb * rb, rb)

        def tile(ref, g, p):
            return ref[g, pl.ds(base, rb), p * LANES:(p + 1) * LANES]

        r_t, w_t, k_t, v_t, a_t, b_t = ([tile(ref, g, p) for g, p in idx]
                                        for ref in (r_ref, w_ref, k_ref, v_ref, a_ref, b_ref))
        states = [s_ref[g, p] for g, p in idx]
        y_tiles = [jnp.zeros((rb, LANES), F32) for _ in idx]
        for i in range(rb):
            row = lambda tiles, j: tiles[j][i:i + 1, :]
            for members in sets:
                sa_in = jnp.concatenate([(states[j] * row(a_t, j)).astype(BF16) for j in members], axis=0)
                vd = jnp.concatenate([jnp.where(diag, row(v_t, j), 0.0).astype(BF16) for j in members], axis=0)
                sa = _dot(sa_in, ones)
                vb = _dot(vd, ones)
                for q, j in enumerate(members):
                    sl = slice(q * A_HEAD, (q + 1) * A_HEAD)
                    states[j] = states[j] * row(w_t, j) + sa[sl] * row(b_t, j) + vb[sl] * row(k_t, j)
                y_in = jnp.concatenate([(states[j] * row(r_t, j)).astype(BF16) for j in members], axis=0)
                yb = _dot(y_in, ones)
                for q, j in enumerate(members):
                    sl = slice(q * A_HEAD, (q + 1) * A_HEAD)
                    y_row = jnp.sum(jnp.where(diag, yb[sl], 0.0), axis=0, keepdims=True)
                    y_tiles[j] = jnp.where(sub == i, y_row, y_tiles[j])
        for j, (g, p) in enumerate(idx):
            s_ref[g, p] = states[j]
            y_ref[g, pl.ds(base, rb), p * LANES:(p + 1) * LANES] = y_tiles[j]
        return carry

    lax.fori_loop(0, tc // rb, block, 0)


def _wkv(seqs, s0, tc, group):
    b, t, d = seqs[0].shape
    pairs = s0.shape[1]
    seq_spec = pl.BlockSpec((group, tc, d), lambda i, j: (i, j, 0))
    st_spec = pl.BlockSpec((group, pairs, A_HEAD, LANES), lambda i, j: (i, 0, 0, 0))
    return pl.pallas_call(
        functools.partial(_wkv_kernel, tc),
        out_shape=[jax.ShapeDtypeStruct((b, t, d), F32), jax.ShapeDtypeStruct(s0.shape, F32)],
        grid=(b // group, t // tc),
        in_specs=[seq_spec] * 6 + [st_spec],
        out_specs=[seq_spec, st_spec],
        compiler_params=_params(("arbitrary", "arbitrary")),
        name="wkv7_scan",
    )(*seqs, s0)


def _mix_out_kernel(rwkv, alpha, d, *refs):
    if rwkv:
        y_ref, g_ref, bonus_ref, x_ref, mod_ref, wo_ref, gn_ref, ln_ref, o_ref = refs
        y = y_ref[...]
        ones = _pair_ones()
        mu_y = _head_sum(y, ones) * (1.0 / A_HEAD)
        dy = y - mu_y
        var_y = _head_sum(dy * dy, ones) * (1.0 / A_HEAD)
        yn = dy * lax.rsqrt(var_y + GN_EPS) * gn_ref[0:1, :] + gn_ref[1:2, :]
        pre = ((yn + bonus_ref[...]) * g_ref[...]).astype(BF16)
    else:
        on_ref, x_ref, mod_ref, wo_ref, ln_ref, o_ref = refs
        pre = on_ref[...].astype(BF16)
    mix = _dot(pre, wo_ref[...])
    gt = mod_ref[0][:, 2 * d:3 * d]
    z = alpha * x_ref[...] + (1.0 + gt) * mix
    o_ref[...] = _layer_norm(z, ln_ref[0:1, :], ln_ref[1:2, :])


def _mix_out(rwkv, alpha, ins, x2d, mod, tiles_per_batch, tm, wo, gn, ln):
    n, d = x2d.shape
    row = pl.BlockSpec((tm, d), lambda i: (i, 0))
    extra = [gn] if rwkv else []
    args = list(ins) + [x2d, mod, wo] + extra + [ln]
    specs = [row] * (len(ins) + 1) + [_mod_spec(mod, tiles_per_batch), _full(wo)] + [_full(e) for e in extra] + [_full(ln)]
    return pl.pallas_call(
        functools.partial(_mix_out_kernel, rwkv, alpha, d),
        out_shape=jax.ShapeDtypeStruct((n, d), F32),
        grid=(n // tm,),
        in_specs=specs,
        out_specs=row,
        compiler_params=_params(("arbitrary",)),
        name="rwkv_out_ln" if rwkv else "attn_out_ln",
    )(*args)


def _mlp_kernel(alpha, d, x_ref, mod_ref, w1_ref, w2_ref, ln_ref, o_ref, hf_ref, acc_ref):
    j = pl.program_id(1)
    mod = mod_ref[0]

    @pl.when(j == 0)
    def _():
        hf_ref[...] = (x_ref[...] * (1.0 + mod[:, 4 * d:5 * d]) + mod[:, 3 * d:4 * d]).astype(BF16)
        acc_ref[...] = jnp.zeros_like(acc_ref)

    h1 = jnp.maximum(_dot(hf_ref[...], w1_ref[...]), 0.0)
    acc_ref[...] += _dot((h1 * h1).astype(BF16), w2_ref[...])

    @pl.when(j == pl.num_programs(1) - 1)
    def _():
        z = alpha * x_ref[...] + (1.0 + mod[:, 5 * d:6 * d]) * acc_ref[...]
        o_ref[...] = _layer_norm(z, ln_ref[0:1, :], ln_ref[1:2, :])


def _mlp(alpha, x2d, mod, tiles_per_batch, tm, w1, w2, ln):
    n, d = x2d.shape
    dff = w1.shape[1]
    tf = 1024
    row = pl.BlockSpec((tm, d), lambda i, j: (i, 0))
    return pl.pallas_call(
        functools.partial(_mlp_kernel, alpha, d),
        out_shape=jax.ShapeDtypeStruct((n, d), F32),
        grid=(n // tm, dff // tf),
        in_specs=[row, _mod_spec(mod, tiles_per_batch),
                  pl.BlockSpec((d, tf), lambda i, j: (0, j)),
                  pl.BlockSpec((tf, d), lambda i, j: (j, 0)),
                  _full(ln)],
        out_specs=row,
        scratch_shapes=[pltpu.VMEM((tm, d), BF16), pltpu.VMEM((tm, d), F32)],
        compiler_params=_params(("arbitrary", "arbitrary")),
        name="sq_relu_mlp_ln",
    )(x2d, mod, w1, w2, ln)


def _qkv_kernel(d, scale, transposed, x_ref, kvmod_ref, mod_ref, wk_ref, wv_ref, wq_ref, k_o, v_o, *extra):
    x = x_ref[...]
    kvmod = kvmod_ref[0]
    hkv = (x * (1.0 + kvmod[:, d:2 * d]) + kvmod[:, 0:d]).astype(BF16)
    k = _dot(hkv, wk_ref[...])
    v = _dot(hkv, wv_ref[...])
    v_o[...] = v
    mod = mod_ref[0]
    h = (x * (1.0 + mod[:, d:2 * d]) + mod[:, 0:d]).astype(BF16)
    q = _dot(h, wq_ref[...]) * scale
    if transposed:
        kb_o, vt_o, qt_o = extra
        k_o[0] = k.T
        kb_o[...] = k.astype(BF16)
        vt_o[0] = v.T.astype(BF16)
        qt_o[0] = q.T.astype(BF16)
    else:
        (q_o,) = extra
        k_o[...] = k
        q_o[...] = q


def _qkv(x2d, kvmod, mod, tiles_per_batch, tm, wk, wv, wq, scale, transposed):
    n, d = x2d.shape
    row = pl.BlockSpec((tm, d), lambda i: (i, 0))
    row_sds = jax.ShapeDtypeStruct((n, d), F32)
    if transposed:
        nb = n // (tiles_per_batch * tm)
        col = pl.BlockSpec((1, d, tm), lambda i: (i // tiles_per_batch, 0, i % tiles_per_batch))
        col_shape = (nb, d, tiles_per_batch * tm)
        t_sds = jax.ShapeDtypeStruct(col_shape, BF16)
        out_shape = [jax.ShapeDtypeStruct(col_shape, F32), row_sds, jax.ShapeDtypeStruct((n, d), BF16), t_sds, t_sds]
        out_specs = [col, row, row, col, col]
    else:
        out_shape = [row_sds] * 3
        out_specs = [row] * 3
    return pl.pallas_call(
        functools.partial(_qkv_kernel, d, scale, transposed),
        out_shape=out_shape,
        grid=(n // tm,),
        in_specs=[row, _mod_spec(kvmod, tiles_per_batch), _mod_spec(mod, tiles_per_batch), _full(wk), _full(wv), _full(wq)],
        out_specs=out_specs,
        compiler_params=_params(("arbitrary",)),
        name="qkv_proj",
    )(x2d, kvmod, mod, wk, wv, wq)


def _diff_lambda(lq_ref, lk_ref, lam_init):
    lq = lq_ref[...]
    lk = lk_ref[...]
    dots = jnp.sum(lq * lk, -1, keepdims=True)
    return jnp.exp(dots[0:1, :]) - jnp.exp(dots[1:2, :]) + lam_init


def _attn_kernel(tq, lam_init, qi_tab, ki_tab, qt_ref, k_ref, vt_ref, lq_ref, lk_ref, g_ref, o_ref,
                 qs_ref, m_ref, l_ref, acc_ref):
    t = pl.program_id(2)
    qi = qi_tab[t]
    ki = ki_tab[t]

    @pl.when(ki == 0)
    def _():
        qt = qt_ref[0]
        row = lax.broadcasted_iota(jnp.int32, qt.shape, 0)
        qs_ref[:, 0:tq] = jnp.where(row < B_HEAD, qt, jnp.zeros_like(qt))
        qs_ref[:, tq:2 * tq] = jnp.where(row >= B_HEAD, qt, jnp.zeros_like(qt))
        m_ref[...] = jnp.full(m_ref.shape, NEG, F32)
        l_ref[...] = jnp.zeros_like(l_ref)
        acc_ref[...] = jnp.zeros_like(acc_ref)

    def update(diagonal):
        s = _dot(k_ref[0], qs_ref[...])
        if diagonal:
            kpos = lax.broadcasted_iota(jnp.int32, s.shape, 0)
            qpos = lax.broadcasted_iota(jnp.int32, s.shape, 1) & (tq - 1)
            s = jnp.where(kpos <= qpos, s, NEG)
        m_prev = m_ref[...]
        m_new = jnp.maximum(m_prev, jnp.max(s, 0, keepdims=True))
        alpha = jnp.exp2(m_prev - m_new)
        p = jnp.exp2(s - m_new)
        l_ref[...] = alpha * l_ref[...] + jnp.sum(p, 0, keepdims=True)
        acc_ref[...] = alpha * acc_ref[...] + _dot(vt_ref[0], p.astype(BF16))
        m_ref[...] = m_new

    @pl.when(ki < qi)
    def _():
        update(False)

    @pl.when(ki == qi)
    def _():
        update(True)
        o_all = acc_ref[...] * (1.0 / l_ref[...])
        lam = _diff_lambda(lq_ref, lk_ref, lam_init)
        o = o_all[:, 0:tq] - lam * o_all[:, tq:2 * tq]
        on = o * lax.rsqrt(jnp.mean(o * o, 0, keepdims=True) + SUBLN_EPS) * g_ref[...] * (1.0 - lam_init)
        o_ref[0] = on.T.astype(o_ref.dtype)


def _diff_attn_prompt(qt, k, vt, lam_q, lam_k, g_col, lam_init, tq):
    b, s, d = k.shape
    heads = d // LANES
    nq = s // tq
    pairs = [(qi, ki) for qi in range(nq) for ki in range(qi + 1)]
    qi_tab = jnp.asarray([p[0] for p in pairs], jnp.int32)
    ki_tab = jnp.asarray([p[1] for p in pairs], jnp.int32)
    qt_spec = pl.BlockSpec((1, LANES, tq), lambda bi, h, t, qt_, kt_: (bi, h, qt_[t]))
    vt_spec = pl.BlockSpec((1, LANES, tq), lambda bi, h, t, qt_, kt_: (bi, h, kt_[t]))
    k_spec = pl.BlockSpec((1, tq, LANES), lambda bi, h, t, qt_, kt_: (bi, kt_[t], h))
    o_spec = pl.BlockSpec((1, tq, LANES), lambda bi, h, t, qt_, kt_: (bi, qt_[t], h))
    small = lambda a: pl.BlockSpec(a.shape, lambda *_: (0,) * a.ndim)
    grid_spec = pltpu.PrefetchScalarGridSpec(
        num_scalar_prefetch=2,
        grid=(b, heads, len(pairs)),
        in_specs=[qt_spec, k_spec, vt_spec, small(lam_q), small(lam_k), small(g_col)],
        out_specs=o_spec,
        scratch_shapes=[pltpu.VMEM((LANES, 2 * tq), BF16), pltpu.VMEM((1, 2 * tq), F32),
                        pltpu.VMEM((1, 2 * tq), F32), pltpu.VMEM((LANES, 2 * tq), F32)],
    )
    return pl.pallas_call(
        functools.partial(_attn_kernel, tq, lam_init),
        out_shape=jax.ShapeDtypeStruct((b, s, d), BF16),
        grid_spec=grid_spec,
        compiler_params=_params(("arbitrary",) * 3),
        name="diff_attn_prompt",
    )(qi_tab, ki_tab, qt, k, vt, lam_q, lam_k, g_col)


def _decode_kernel(pp, lam_init, heads, pt_ref, qcol_ref, qrow_ref, kn_ref, vn_ref, lq_ref, lk_ref, g_ref, *rest):
    k_refs = rest[:pp]
    v_refs = rest[pp:2 * pp]
    o_ref, qb_ref, m_ref, l_ref, acc_ref = rest[2 * pp:]
    step = pl.program_id(1)
    rows = 2 * heads
    d = qb_ref.shape[0]

    @pl.when(step == 0)
    def _():
        qb_ref[...] = jnp.broadcast_to(qcol_ref[0], qb_ref.shape)
        m_ref[...] = jnp.full(m_ref.shape, NEG, F32)
        l_ref[...] = jnp.zeros_like(l_ref)
        acc_ref[...] = jnp.zeros_like(acc_ref)

    r = lax.broadcasted_iota(jnp.int32, (rows, d), 0)
    c = lax.broadcasted_iota(jnp.int32, (rows, d), 1)
    sel = ((c >> 7) == (r & (heads - 1))) & (((c >> 6) & 1) == (r // heads))
    sel_b = jnp.where(sel, 1.0, 0.0).astype(BF16)
    row_head = lax.broadcasted_iota(jnp.int32, (rows, LANES), 0) & (heads - 1)

    s = jnp.concatenate([_dot(sel_b, (k_refs[i][0] * qb_ref[...]).astype(BF16)) for i in range(pp)], axis=1)
    m_prev = m_ref[...]
    m_new = jnp.maximum(m_prev, jnp.max(s, -1, keepdims=True))
    alpha = jnp.exp(m_prev - m_new)
    p = jnp.exp(s - m_new)
    l_ref[...] = alpha * l_ref[...] + jnp.sum(p, -1, keepdims=True)
    pb = p.astype(BF16)
    pv = jnp.zeros((rows, LANES), F32)
    for i in range(pp):
        pb_i = pb[:, i * PAGE_SIZE:(i + 1) * PAGE_SIZE]
        for h in range(heads):
            v_h = v_refs[i][0, pl.ds(h, PAGE_SIZE, stride=heads), :].astype(BF16)
            pv = pv + _dot(jnp.where(row_head == h, pb_i, jnp.zeros_like(pb_i)), v_h)
    acc_ref[...] = alpha * acc_ref[...] + pv
    m_ref[...] = m_new

    @pl.when(step == pl.num_programs(1) - 1)
    def _():
        s_new = jnp.sum(jnp.where(sel, qrow_ref[0] * kn_ref[0], 0.0), -1, keepdims=True)
        m_prev = m_ref[...]
        m_fin = jnp.maximum(m_prev, s_new)
        alpha = jnp.exp(m_prev - m_fin)
        p_new = jnp.exp(s_new - m_fin)
        l_fin = alpha * l_ref[...] + p_new
        o_all = (alpha * acc_ref[...] + p_new * vn_ref[0]) / l_fin
        lam = _diff_lambda(lq_ref, lk_ref, lam_init)
        o = o_all[0:heads] - lam * o_all[heads:rows]
        on = o * lax.rsqrt(jnp.mean(o * o, -1, keepdims=True) + SUBLN_EPS) * g_ref[...] * (1.0 - lam_init)
        o_ref[0] = on


def _diff_attn_decode(q, k_new, v_new, cache_kt, cache_v, page_table, lam_q, lam_k, g_row, lam_init, pp):
    bd, d = q.shape
    heads = d // LANES
    rows = 2 * heads
    n_pages = page_table.shape[1]
    at_b = lambda shape: pl.BlockSpec((1,) + shape, lambda b, s, pt: (b, 0, 0))

    def page_spec(shape, i):
        return pl.BlockSpec((1,) + shape, lambda b, s, pt: (pt[b, s * pp + i], 0, 0))

    small = lambda a: pl.BlockSpec(a.shape, lambda b, s, pt: (0,) * a.ndim)
    grid_spec = pltpu.PrefetchScalarGridSpec(
        num_scalar_prefetch=1,
        grid=(bd, n_pages // pp),
        in_specs=[at_b((d, 1)), at_b((1, d)), at_b((1, d)), at_b((rows, LANES)), small(lam_q), small(lam_k), small(g_row)]
                 + [page_spec((d, PAGE_SIZE), i) for i in range(pp)]
                 + [page_spec((PAGE_SIZE * heads, LANES), i) for i in range(pp)],
        out_specs=at_b((heads, LANES)),
        scratch_shapes=[pltpu.VMEM((d, PAGE_SIZE), F32), pltpu.VMEM((rows, 1), F32), pltpu.VMEM((rows, 1), F32),
                        pltpu.VMEM((rows, LANES), F32)],
    )
    v_rows = jnp.tile(v_new, (1, 2, 1))
    return pl.pallas_call(
        functools.partial(_decode_kernel, pp, lam_init, heads),
        out_shape=jax.ShapeDtypeStruct((bd, heads, LANES), F32),
        grid_spec=grid_spec,
        compiler_params=_params(("arbitrary", "arbitrary")),
        name="diff_attn_decode",
    )(page_table, q.reshape(bd, d, 1), q.reshape(bd, 1, d), k_new.reshape(bd, 1, d), v_rows, lam_q, lam_k, g_row,
      *([cache_kt] * pp), *([cache_v] * pp))


def _pad_to(a, axis, size):
    pad = [(0, 0)] * a.ndim
    pad[axis] = (0, size - a.shape[axis])
    return jnp.pad(a, pad)


def _pack_state(s):
    b, h, n, _ = s.shape
    return s.reshape(b, h // 2, 2, n, n).transpose(0, 1, 3, 2, 4).reshape(b, h // 2, n, 2 * n)


def _unpack_state(s):
    b, p, n, _ = s.shape
    return s.reshape(b, p, n, 2, n).transpose(0, 1, 3, 2, 4).reshape(b, 2 * p, n, n)


def kernel(x_prompt, x_sample, cache_k, cache_v, state_wkv, state_shift, page_table, c_prompt, c_sample, ln_g, ln_b, ada_w, ada_b, ffn_w1, ffn_w2, a_mu, a_w_rkv, a_w_o, a_w0, a_w1, a_w2, a_a0, a_a1, a_a2, a_g1, a_g2, a_k_k, a_k_a, a_r_k, a_gn_g, a_gn_b, kv_ada_w, kv_ada_b, kv_w_k, kv_w_v, b_w_q, b_w_o, b_lam_q, b_lam_k, b_subln_g):
    bp, seq, d = x_prompt.shape
    bd, dec_seq, _ = x_sample.shape
    depth = ada_w.shape[0]
    n_a = a_mu.shape[0]
    assert depth == 2 and n_a == 1 and dec_seq == 1
    alpha = (2 * depth) ** 0.25
    heads = d // LANES
    pool = cache_k.shape[0]

    n_c = bp + bd
    c_all = _pad_to(jnp.concatenate([c_prompt, c_sample], axis=0), 0, -(-n_c // 8) * 8)
    mods = [_ada(c_all, ada_w[l], ada_b[l]) for l in range(depth)]
    kvmods = _ada(c_all, kv_ada_w, kv_ada_b)

    wrkv = a_w_rkv[0].astype(BF16)
    lora = 128
    w1 = _pad_to(a_w1[0], 1, lora).astype(BF16)
    w2 = _pad_to(a_w2[0], 0, lora).astype(BF16)
    a1 = _pad_to(a_a1[0], 1, lora).astype(BF16)
    a2 = _pad_to(a_a2[0], 0, lora).astype(BF16)
    g1 = _pad_to(a_g1[0], 1, 2 * lora).astype(BF16)
    g2 = _pad_to(a_g2[0], 0, 2 * lora).astype(BF16)
    vec = _pad_to(jnp.stack([a_w0[0], a_a0[0], a_k_k[0], a_k_a[0], a_r_k[0].reshape(d)]), 0, 8)
    gn = jnp.stack([a_gn_g[0], a_gn_b[0]])
    a_wo = a_w_o[0].astype(BF16)
    ffn1 = ffn_w1.astype(BF16)
    ffn2 = ffn_w2.astype(BF16)
    wk = kv_w_k.astype(BF16)
    wv = kv_w_v.astype(BF16)
    wq = b_w_q[0].astype(BF16)
    b_wo = b_w_o[0].astype(BF16)
    g_head = b_subln_g[0].reshape(1, LANES)
    lam_init = 0.8 - 0.6 * math.exp(-0.3 * n_a)
    ln = [[jnp.stack([ln_g[l, s], ln_b[l, s]]) for s in range(2)] for l in range(depth)]
    attn_scale = B_HEAD ** -0.5

    def trunk(x, rows, shift0, wkv0, seq_len, tm, tm_mlp, tc, attend):
        b = x.shape[0]
        x2d = x.reshape(b * seq_len, d)
        tiles = max(seq_len // tm, 1)
        tiles_mlp = max(seq_len // tm_mlp, 1)
        if seq_len == 1:
            shape_mod = lambda m: m[rows].reshape(1, b, -1)
            sh0 = shift0.reshape(1, b, d)
        else:
            shape_mod = lambda m: m[rows].reshape(b, 1, -1)
            sh0 = shift0.reshape(b, 1, d)
        mod0, mod1, kvmod = shape_mod(mods[0]), shape_mod(mods[1]), shape_mod(kvmods)

        *seqs, g, bonus, last = _rwkv_proj(x2d, mod0, sh0, seq_len, tm, a_mu[0], wrkv, w1, w2, a1, a2, g1, g2, vec)
        seqs = [s.reshape(b, seq_len, d) for s in seqs]
        y, s_fin = _wkv(seqs, _pack_state(wkv0), tc, 2)
        x1 = _mix_out(True, alpha, [y.reshape(b * seq_len, d), g, bonus], x2d, mod0, tiles, tm, a_wo, gn, ln[0][0])
        x2 = _mlp(alpha, x1, mod0, tiles_mlp, tm_mlp, ffn1[0], ffn2[0], ln[0][1])
        if seq_len > 1:
            kt, v, kb, vt, qt = _qkv(x2, kvmod, mod1, tiles, tm, wk, wv, wq, attn_scale * LOG2E, True)
            k_out = kt.reshape(b, heads, 2, B_HEAD, seq_len).transpose(0, 4, 1, 2, 3)
            on = attend(kb, vt, qt)
        else:
            k, v, q = _qkv(x2, kvmod, mod1, tiles, tm, wk, wv, wq, attn_scale, False)
            k_out = k.reshape(b, seq_len, heads, 2, B_HEAD)
            on = attend(k, v, q)
        x3 = _mix_out(False, alpha, [on], x2, mod1, tiles, tm, b_wo, gn, ln[1][0])
        y_out = _mlp(alpha, x3, mod1, tiles_mlp, tm_mlp, ffn1[1], ffn2[1], ln[1][1])
        return (y_out.reshape(b, seq_len, d), k_out, v.reshape(b, seq_len, heads, 2 * B_HEAD),
                _unpack_state(s_fin)[None], last.reshape(1, b, d))

    def attend_prompt(kb, vt, qt):
        on = _diff_attn_prompt(qt, kb.reshape(bp, seq, d), vt, b_lam_q[0], b_lam_k[0], g_head.reshape(LANES, 1),
                               lam_init, min(1024, seq))
        return on.reshape(bp * seq, d)

    def attend_sample(k, v, q):
        cache_kt = jnp.transpose(cache_k, (0, 2, 3, 4, 1)).reshape(pool, d, PAGE_SIZE)
        on = _diff_attn_decode(q, k, v.reshape(bd, heads, LANES), cache_kt,
                               cache_v.reshape(pool, PAGE_SIZE * heads, LANES), page_table,
                               b_lam_q[0], b_lam_k[0], g_head, lam_init, min(8, page_table.shape[1]))
        return on.reshape(bd, d)

    zeros_shift = jnp.zeros((bp, d), F32)
    zeros_wkv = jnp.zeros((bp, d // A_HEAD, A_HEAD, A_HEAD), F32)
    y_p, k_p, v_p, wkv_p, shift_p = trunk(x_prompt, slice(0, bp), zeros_shift, zeros_wkv, seq,
                                          min(256, seq), min(512, seq), min(128, seq), attend_prompt)
    y_s, k_s, v_s, wkv_s, shift_s = trunk(x_sample, slice(bp, bp + bd), state_shift[0], state_wkv[0], 1,
                                          bd, bd, 1, attend_sample)
    return (y_p, y_s, k_p, v_p, wkv_p, shift_p, k_s, v_s, wkv_s, shift_s)
```

```python
import functools
import math

import jax
import jax.numpy as jnp
from jax import lax
from jax.experimental import pallas as pl
from jax.experimental.pallas import tpu as pltpu

F32 = jnp.float32
BF16 = jnp.bfloat16

LANES = 128
A_HEAD = 64
B_HEAD = 64
PAGE_SIZE = 128
GN_EPS = 64e-5
SUBLN_EPS = 1e-5
LN_EPS = 1e-5
NEG = -1e30
LOG2E = math.log2(math.e)
VMEM_LIMIT = 56 * 2**20


def _params(sem):
    return pltpu.CompilerParams(dimension_semantics=sem, vmem_limit_bytes=VMEM_LIMIT)


def _dot(a, b):
    return jnp.dot(a, b, preferred_element_type=F32)


def _split(x):
    hi = x.astype(BF16)
    lo = (x - hi.astype(F32)).astype(BF16)
    return hi, lo


def _sigmoid(z):
    return 1.0 / (1.0 + jnp.exp(-z))


def _layer_norm(z, g, b):
    mu = jnp.mean(z, -1, keepdims=True)
    d = z - mu
    var = jnp.mean(d * d, -1, keepdims=True)
    return d * lax.rsqrt(var + LN_EPS) * g + b


def _pair_ones():
    r = lax.broadcasted_iota(jnp.int32, (LANES, LANES), 0) >> 6
    c = lax.broadcasted_iota(jnp.int32, (LANES, LANES), 1) >> 6
    return jnp.where(r == c, 1.0, 0.0).astype(BF16)


def _head_sum(x, ones):
    m, d = x.shape
    cols = d // LANES
    stacked = jnp.concatenate([x[:, c * LANES:(c + 1) * LANES] for c in range(cols)], axis=0)
    hi, lo = _split(stacked)
    s = _dot(hi, ones) + _dot(lo, ones)
    return jnp.concatenate([s[c * m:(c + 1) * m] for c in range(cols)], axis=1)


def _mod_spec(mod, tiles_per_batch):
    return pl.BlockSpec((1, mod.shape[1], mod.shape[2]), lambda i, *_: (i // tiles_per_batch, 0, 0))


def _full(a):
    nd = a.ndim
    return pl.BlockSpec(a.shape, lambda *_: (0,) * nd)


def _ada_kernel(c_ref, w_ref, b_ref, o_ref):
    c = c_ref[...]
    s = c * _sigmoid(c)
    o_ref[...] = _dot(s.astype(BF16), w_ref[...].astype(BF16)) + b_ref[...]


def _ada(c, w, b):
    m, d = c.shape
    n = w.shape[1]
    tn = 1024
    return pl.pallas_call(
        _ada_kernel,
        out_shape=jax.ShapeDtypeStruct((m, n), F32),
        grid=(n // tn,),
        in_specs=[pl.BlockSpec((m, d), lambda j: (0, 0)),
                  pl.BlockSpec((d, tn), lambda j: (0, j)),
                  pl.BlockSpec((1, tn), lambda j: (0, j))],
        out_specs=pl.BlockSpec((m, tn), lambda j: (0, j)),
        compiler_params=_params(("arbitrary",)),
        name="ada_proj",
    )(c, w, b.reshape(1, n))


def _rwkv_proj_kernel(single_step, tiles_per_batch, d,
                      x_ref, xp_ref, mod_ref, sh0_ref, mu_ref, wrkv_ref, w1_ref, w2_ref, a1_ref, a2_ref,
                      g1_ref, g2_ref, vec_ref,
                      r_o, w_o, k_o, v_o, a_o, b_o, g_o, bonus_o, last_o):
    mod = mod_ref[0]
    sh = mod[:, 0:d]
    sc = mod[:, d:2 * d]
    x = x_ref[...]
    h = x * (1.0 + sc) + sh
    if single_step:
        h_prev = sh0_ref[0]
        last_o[0] = h
    else:
        i = pl.program_id(0)
        prev_row = xp_ref[7:8, :] * (1.0 + sc) + sh
        first = jnp.where(i % tiles_per_batch == 0, sh0_ref[0], prev_row)
        rolled = pltpu.roll(h, 1, 0)
        rid = lax.broadcasted_iota(jnp.int32, h.shape, 0)
        h_prev = jnp.where(rid == 0, first, rolled)
        last_o[0] = h[h.shape[0] - 1:, :]
    xx = h_prev - h

    def mix(p):
        return (h + xx * mu_ref[p:p + 1, :]).astype(BF16)

    r = _dot(mix(0), wrkv_ref[0])
    k = _dot(mix(1), wrkv_ref[1])
    v = _dot(mix(2), wrkv_ref[2])
    lw = _dot(jnp.tanh(_dot(mix(3), w1_ref[...])).astype(BF16), w2_ref[...])
    la = _dot(_dot(mix(4), a1_ref[...]).astype(BF16), a2_ref[...])
    g = _dot(_sigmoid(_dot(mix(5), g1_ref[...])).astype(BF16), g2_ref[...])

    w0 = vec_ref[0:1, :]
    a0 = vec_ref[1:2, :]
    k_k = vec_ref[2:3, :]
    k_a = vec_ref[3:4, :]
    r_k = vec_ref[4:5, :]

    z = -(w0 + lw)
    softplus = jnp.maximum(z, 0.0) + jnp.log(1.0 + jnp.exp(-jnp.abs(z)))
    w_log = -softplus - 0.5
    log_decay = -jnp.exp(w_log)
    a_gate = _sigmoid(a0 + la)
    ones = _pair_ones()
    kkf = k * k_k
    norm = jnp.sqrt(_head_sum(kkf * kkf, ones))
    kk = kkf / jnp.maximum(norm, 1e-12)
    k_mod = k * (1.0 + (a_gate - 1.0) * k_a)
    bonus = _head_sum(r * k_mod * r_k, ones) * v

    r_o[...] = r
    w_o[...] = jnp.exp(log_decay) if single_step else log_decay
    k_o[...] = k_mod
    v_o[...] = v
    a_o[...] = -kk
    b_o[...] = kk * a_gate
    g_o[...] = g
    bonus_o[...] = bonus


def _rwkv_proj(x2d, mod, shift0, seq_len, tm, mu, wrkv, w1, w2, a1, a2, g1, g2, vec):
    n, d = x2d.shape
    single = seq_len == 1
    tiles_per_batch = 1 if single else seq_len // tm
    nt = n // tm
    nb = mod.shape[0]
    last_rows = tm if single else 1
    row = pl.BlockSpec((tm, d), lambda i: (i, 0))
    if single:
        xp_spec = pl.BlockSpec((8, d), lambda i: (0, 0))
        sh0_spec = pl.BlockSpec((1, tm, d), lambda i: (i, 0, 0))
    else:
        xp_spec = pl.BlockSpec((8, d), lambda i: (jnp.maximum(i * (tm // 8) - 1, 0), 0))
        sh0_spec = pl.BlockSpec((1, 1, d), lambda i: (i // tiles_per_batch, 0, 0))
    out_sds = jax.ShapeDtypeStruct((n, d), F32)
    outs = pl.pallas_call(
        functools.partial(_rwkv_proj_kernel, single, tiles_per_batch, d),
        out_shape=[out_sds] * 8 + [jax.ShapeDtypeStruct((nb, last_rows, d), F32)],
        grid=(nt,),
        in_specs=[row, xp_spec, _mod_spec(mod, tiles_per_batch), sh0_spec, _full(mu), _full(wrkv), _full(w1),
                  _full(w2), _full(a1), _full(a2), _full(g1), _full(g2), _full(vec)],
        out_specs=[row] * 8 + [pl.BlockSpec((1, last_rows, d), lambda i: (i // tiles_per_batch, 0, 0))],
        compiler_params=_params(("arbitrary",)),
        name="rwkv_proj",
    )(x2d, x2d, mod, shift0, mu, wrkv, w1, w2, a1, a2, g1, g2, vec)
    return outs


def _wkv_kernel(tc, r_ref, w_ref, k_ref, v_ref, a_ref, b_ref, s0_ref, y_ref, s_ref):
    @pl.when(pl.program_id(1) == 0)
    def _():
        s_ref[...] = s0_ref[...]

    groups, pairs = s_ref.shape[0], s_ref.shape[1]
    ones = _pair_ones()
    rid = lax.broadcasted_iota(jnp.int32, (A_HEAD, LANES), 0)
    cid = lax.broadcasted_iota(jnp.int32, (A_HEAD, LANES), 1) & (A_HEAD - 1)
    diag = rid == cid
    idx = [(g, p) for g in range(groups) for p in range(pairs)]
    n = len(idx)
    sets = [list(range(n))]

    rb = min(8, tc)
    sub = lax.broadcasted_iota(jnp.int32, (rb, LANES), 0)

    def block(tb, carry):
        base = pl.multiple_of(tb * rb, rb)

        def tile(ref, g, p):
            return ref[g, pl.ds(base, rb), p * LANES:(p + 1) * LANES]

        r_t, w_t, k_t, v_t, a_t, b_t = ([tile(ref, g, p) for g, p in idx]
                                        for ref in (r_ref, w_ref, k_ref, v_ref, a_ref, b_ref))
        states = [s_ref[g, p] for g, p in idx]
        y_tiles = [jnp.zeros((rb, LANES), F32) for _ in idx]
        for i in range(rb):
            row = lambda tiles, j: tiles[j][i:i + 1, :]
            for members in sets:
                sa_in = jnp.concatenate([(states[j] * row(a_t, j)).astype(BF16) for j in members], axis=0)
                vd = jnp.concatenate([jnp.where(diag, row(v_t, j), 0.0).astype(BF16) for j in members], axis=0)
                sa = _dot(sa_in, ones)
                vb = _dot(vd, ones)
                for q, j in enumerate(members):
                    sl = slice(q * A_HEAD, (q + 1) * A_HEAD)
                    states[j] = states[j] * row(w_t, j) + sa[sl] * row(b_t, j) + vb[sl] * row(k_t, j)
                y_in = jnp.concatenate([(states[j] * row(r_t, j)).astype(BF16) for j in members], axis=0)
                yb = _dot(y_in, ones)
                for q, j in enumerate(members):
                    sl = slice(q * A_HEAD, (q + 1) * A_HEAD)
                    y_row = jnp.sum(jnp.where(diag, yb[sl], 0.0), axis=0, keepdims=True)
                    y_tiles[j] = jnp.where(sub == i, y_row, y_tiles[j])
        for j, (g, p) in enumerate(idx):
            s_ref[g, p] = states[j]
            y_ref[g, pl.ds(base, rb), p * LANES:(p + 1) * LANES] = y_tiles[j]
        return carry

    lax.fori_loop(0, tc // rb, block, 0)


def _wkv(seqs, s0, tc, group):
    b, t, d = seqs[0].shape
    pairs = s0.shape[1]
    seq_spec = pl.BlockSpec((group, tc, d), lambda i, j: (i, j, 0))
    st_spec = pl.BlockSpec((group, pairs, A_HEAD, LANES), lambda i, j: (i, 0, 0, 0))
    return pl.pallas_call(
        functools.partial(_wkv_kernel, tc),
        out_shape=[jax.ShapeDtypeStruct((b, t, d), F32), jax.ShapeDtypeStruct(s0.shape, F32)],
        grid=(b // group, t // tc),
        in_specs=[seq_spec] * 6 + [st_spec],
        out_specs=[seq_spec, st_spec],
        compiler_params=_params(("arbitrary", "arbitrary")),
        name="wkv7_scan",
    )(*seqs, s0)


WKV_CHUNK = 64


def _dot_nt(a, b):
    return lax.dot_general(a, b, (((1,), (1,)), ((), ())), preferred_element_type=F32)


def _dot_tn(a, b):
    return lax.dot_general(a, b, (((0,), (0,)), ((), ())), preferred_element_type=F32)


def _mm3(a, b):
    ah, al = _split(a)
    bh, bl = _split(b)
    return _dot(ah, bh) + _dot(ah, bl) + _dot(al, bh)


def _wkv_chunk_kernel(r_ref, lw_ref, k_ref, v_ref, a_ref, b_ref, s0_ref, y_ref, s_ref):
    @pl.when(pl.program_id(1) == 0)
    def _():
        s_ref[...] = s0_ref[...]

    groups, pairs = s_ref.shape[0], s_ref.shape[1]
    c = r_ref.shape[1]
    iota = lambda shape, dim: lax.broadcasted_iota(jnp.int32, shape, dim)
    head0 = iota((c, LANES), 1) < A_HEAD
    tri = jnp.where(iota((c, c), 1) <= iota((c, c), 0), 1.0, 0.0).astype(BF16)
    rows4 = iota((4 * c, LANES), 0)
    t_idx = rows4 & (c - 1)
    s_idx = iota((4 * c, LANES), 1) & (c - 1)
    keep = s_idx < t_idx + jnp.where(rows4 < 2 * c, 0, 1)
    same_head = (iota((LANES, LANES), 0) >> 6) == (iota((LANES, LANES), 1) >> 6)
    same_block = (iota((LANES, LANES), 0) >> 3) == (iota((LANES, LANES), 1) >> 3)
    eye = jnp.where(iota((LANES, LANES), 0) == iota((LANES, LANES), 1), 1.0, 0.0)

    def by_head(x):
        return jnp.concatenate([jnp.where(head0, x, 0.0), jnp.where(head0, 0.0, x)], axis=0)

    units = [(g, p) for g in range(groups) for p in range(pairs)]
    each = lambda f, *lists: [f(*args) for args in zip(*lists)]
    bf = lambda xs: [x.astype(BF16) for x in xs]
    swap = lambda x: pltpu.roll(x, A_HEAD, 1)

    def load(ref):
        return [ref[g, :, p * LANES:(p + 1) * LANES] for g, p in units]

    r, lw, k, v, a, b = (load(ref) for ref in (r_ref, lw_ref, k_ref, v_ref, a_ref, b_ref))
    s_old = [s_ref[g, p] for g, p in units]

    def cumulative(x):
        hi, lo = _split(x)
        return _dot(tri, hi) + _dot(tri, lo)

    cum = each(cumulative, lw)
    end = [x[c - 1:c, :] for x in cum]
    inv = [jnp.exp(-x) for x in cum]
    rest = each(lambda e, x: jnp.exp(e - x), end, cum)
    at = each(lambda a_, x, l_: a_ * jnp.exp(x - l_), a, cum, lw)
    rt = each(lambda r_, x: r_ * jnp.exp(x), r, cum)
    bt = each(jnp.multiply, b, inv)
    kt = each(jnp.multiply, k, inv)
    bb = bf(each(jnp.multiply, b, rest))
    kb = bf(each(jnp.multiply, k, rest))
    at_b, rt_b, v_b = bf(at), bf(rt), bf(v)
    lhs = each(lambda x, y: jnp.concatenate([by_head(x), by_head(y)], axis=0).astype(BF16), at, rt)
    rhs = each(lambda x, y: jnp.concatenate([x, y], axis=0).astype(BF16), bt, kt)
    aa = each(lambda x, y: jnp.where(keep, _dot_nt(x, y), 0.0), lhs, rhs)
    a0, a1, r0, r1 = ([x[i * c:(i + 1) * c] for x in aa] for i in range(4))
    n = each(lambda x, y: jnp.concatenate([jnp.where(head0, x, 0.0), jnp.where(head0, 0.0, swap(y))], axis=0), a0, a1)
    a_ak = bf(each(lambda x, y: jnp.where(head0, swap(x), y), a0, a1))
    a_rb = bf(each(lambda x, y: jnp.where(head0, x, swap(y)), r0, r1))
    a_rk = bf(each(lambda x, y: jnp.where(head0, swap(x), y), r0, r1))
    def neumann8(z):
        out = [eye + z_ for z_ in z]
        for _ in range(2):
            z = each(_mm3, z, z)
            out = each(lambda o_, z_: o_ + _mm3(o_, z_), out, z)
        return out

    d_in = [jnp.where(same_block, n_, 0.0) for n_ in n]
    d_inv = neumann8(d_in)
    m_mat = each(lambda di_, n_, d_: _mm3(di_, n_ - d_), d_inv, n, d_in)
    inv_t = each(_mm3, neumann8(m_mat), d_inv)
    v_heads = bf(each(by_head, v))
    s_b = bf(s_old)
    rhs0 = each(lambda at_, s_, ak_, vh_: _dot_nt(at_, s_) + _dot(ak_, vh_), at_b, s_b, a_ak, v_heads)
    uu = each(lambda t_, x_: _dot(t_.astype(BF16), by_head(x_).astype(BF16)), inv_t, rhs0)
    u = [x_[0:c] + x_[c:2 * c] for x_ in uu]
    u_heads = bf(each(by_head, u))
    y = each(lambda rt_, s_, rb_, uh_, rk_, vh_: _dot_nt(rt_, s_) + _dot(rb_, uh_) + _dot(rk_, vh_),
             rt_b, s_b, a_rb, u_heads, a_rk, v_heads)
    upd = each(lambda u_, bb_, v_, kb_: _dot_tn(u_, bb_) + _dot_tn(v_, kb_), bf(u), bb, v_b, kb)
    s_new = each(lambda s_, e, up: s_ * jnp.exp(e) + jnp.where(same_head, up, 0.0), s_old, end, upd)
    for (g, p), y_, s_ in zip(units, y, s_new):
        y_ref[g, :, p * LANES:(p + 1) * LANES] = y_
        s_ref[g, p] = s_


def _wkv_chunked(seqs, s0, group):
    b, t, d = seqs[0].shape
    pairs = s0.shape[1]
    seq_spec = pl.BlockSpec((group, WKV_CHUNK, d), lambda i, j: (i, j, 0))
    st_spec = pl.BlockSpec((group, pairs, LANES, LANES), lambda i, j: (i, 0, 0, 0))
    return pl.pallas_call(
        _wkv_chunk_kernel,
        out_shape=[jax.ShapeDtypeStruct((b, t, d), F32), jax.ShapeDtypeStruct(s0.shape, F32)],
        grid=(b // group, t // WKV_CHUNK),
        in_specs=[seq_spec] * 6 + [st_spec],
        out_specs=[seq_spec, st_spec],
        compiler_params=_params(("arbitrary", "arbitrary")),
        name="wkv7_chunked",
    )(*seqs, s0)


def _pair_block_diag(s):
    b, h, n, _ = s.shape
    s = s.reshape(b, h // 2, 2, n, n)
    z = jnp.zeros_like(s[:, :, 0])
    top = jnp.concatenate([s[:, :, 0], z], axis=-1)
    bot = jnp.concatenate([z, s[:, :, 1]], axis=-1)
    return jnp.concatenate([top, bot], axis=-2)


def _pair_diag_blocks(s):
    b, p, _, _ = s.shape
    n = A_HEAD
    return jnp.stack([s[:, :, 0:n, 0:n], s[:, :, n:, n:]], axis=2).reshape(b, 2 * p, n, n)


def _mix_out_kernel(rwkv, alpha, d, *refs):
    if rwkv:
        y_ref, g_ref, bonus_ref, x_ref, mod_ref, wo_ref, gn_ref, ln_ref, o_ref = refs
        y = y_ref[...]
        ones = _pair_ones()
        mu_y = _head_sum(y, ones) * (1.0 / A_HEAD)
        dy = y - mu_y
        var_y = _head_sum(dy * dy, ones) * (1.0 / A_HEAD)
        yn = dy * lax.rsqrt(var_y + GN_EPS) * gn_ref[0:1, :] + gn_ref[1:2, :]
        pre = ((yn + bonus_ref[...]) * g_ref[...]).astype(BF16)
    else:
        on_ref, x_ref, mod_ref, wo_ref, ln_ref, o_ref = refs
        pre = on_ref[...].astype(BF16)
    mix = _dot(pre, wo_ref[...])
    gt = mod_ref[0][:, 2 * d:3 * d]
    z = alpha * x_ref[...] + (1.0 + gt) * mix
    o_ref[...] = _layer_norm(z, ln_ref[0:1, :], ln_ref[1:2, :])


def _mix_out(rwkv, alpha, ins, x2d, mod, tiles_per_batch, tm, wo, gn, ln):
    n, d = x2d.shape
    row = pl.BlockSpec((tm, d), lambda i: (i, 0))
    extra = [gn] if rwkv else []
    args = list(ins) + [x2d, mod, wo] + extra + [ln]
    specs = [row] * (len(ins) + 1) + [_mod_spec(mod, tiles_per_batch), _full(wo)] + [_full(e) for e in extra] + [_full(ln)]
    return pl.pallas_call(
        functools.partial(_mix_out_kernel, rwkv, alpha, d),
        out_shape=jax.ShapeDtypeStruct((n, d), F32),
        grid=(n // tm,),
        in_specs=specs,
        out_specs=row,
        compiler_params=_params(("arbitrary",)),
        name="rwkv_out_ln" if rwkv else "attn_out_ln",
    )(*args)


def _mlp_kernel(alpha, d, x_ref, mod_ref, w1_ref, w2_ref, ln_ref, o_ref, hf_ref, acc_ref):
    j = pl.program_id(1)
    mod = mod_ref[0]

    @pl.when(j == 0)
    def _():
        hf_ref[...] = (x_ref[...] * (1.0 + mod[:, 4 * d:5 * d]) + mod[:, 3 * d:4 * d]).astype(BF16)
        acc_ref[...] = jnp.zeros_like(acc_ref)

    h1 = jnp.maximum(_dot(hf_ref[...], w1_ref[...]), 0.0)
    acc_ref[...] += _dot((h1 * h1).astype(BF16), w2_ref[...])

    @pl.when(j == pl.num_programs(1) - 1)
    def _():
        z = alpha * x_ref[...] + (1.0 + mod[:, 5 * d:6 * d]) * acc_ref[...]
        o_ref[...] = _layer_norm(z, ln_ref[0:1, :], ln_ref[1:2, :])


def _mlp(alpha, x2d, mod, tiles_per_batch, tm, w1, w2, ln):
    n, d = x2d.shape
    dff = w1.shape[1]
    tf = 1024
    row = pl.BlockSpec((tm, d), lambda i, j: (i, 0))
    return pl.pallas_call(
        functools.partial(_mlp_kernel, alpha, d),
        out_shape=jax.ShapeDtypeStruct((n, d), F32),
        grid=(n // tm, dff // tf),
        in_specs=[row, _mod_spec(mod, tiles_per_batch),
                  pl.BlockSpec((d, tf), lambda i, j: (0, j)),
                  pl.BlockSpec((tf, d), lambda i, j: (j, 0)),
                  _full(ln)],
        out_specs=row,
        scratch_shapes=[pltpu.VMEM((tm, d), BF16), pltpu.VMEM((tm, d), F32)],
        compiler_params=_params(("arbitrary", "arbitrary")),
        name="sq_relu_mlp_ln",
    )(x2d, mod, w1, w2, ln)


def _qkv_kernel(d, scale, transposed, x_ref, kvmod_ref, mod_ref, wk_ref, wv_ref, wq_ref, k_o, v_o, *extra):
    x = x_ref[...]
    kvmod = kvmod_ref[0]
    hkv = (x * (1.0 + kvmod[:, d:2 * d]) + kvmod[:, 0:d]).astype(BF16)
    k = _dot(hkv, wk_ref[...])
    v = _dot(hkv, wv_ref[...])
    v_o[...] = v
    mod = mod_ref[0]
    h = (x * (1.0 + mod[:, d:2 * d]) + mod[:, 0:d]).astype(BF16)
    q = _dot(h, wq_ref[...]) * scale
    if transposed:
        kb_o, vt_o, qt_o = extra
        k_o[0] = k.T
        kb_o[...] = k.astype(BF16)
        vt_o[0] = v.T.astype(BF16)
        qt_o[0] = q.T.astype(BF16)
    else:
        (q_o,) = extra
        k_o[...] = k
        q_o[...] = q


def _qkv(x2d, kvmod, mod, tiles_per_batch, tm, wk, wv, wq, scale, transposed):
    n, d = x2d.shape
    row = pl.BlockSpec((tm, d), lambda i: (i, 0))
    row_sds = jax.ShapeDtypeStruct((n, d), F32)
    if transposed:
        nb = n // (tiles_per_batch * tm)
        col = pl.BlockSpec((1, d, tm), lambda i: (i // tiles_per_batch, 0, i % tiles_per_batch))
        col_shape = (nb, d, tiles_per_batch * tm)
        t_sds = jax.ShapeDtypeStruct(col_shape, BF16)
        out_shape = [jax.ShapeDtypeStruct(col_shape, F32), row_sds, jax.ShapeDtypeStruct((n, d), BF16), t_sds, t_sds]
        out_specs = [col, row, row, col, col]
    else:
        out_shape = [row_sds] * 3
        out_specs = [row] * 3
    return pl.pallas_call(
        functools.partial(_qkv_kernel, d, scale, transposed),
        out_shape=out_shape,
        grid=(n // tm,),
        in_specs=[row, _mod_spec(kvmod, tiles_per_batch), _mod_spec(mod, tiles_per_batch), _full(wk), _full(wv), _full(wq)],
        out_specs=out_specs,
        compiler_params=_params(("arbitrary",)),
        name="qkv_proj",
    )(x2d, kvmod, mod, wk, wv, wq)


def _diff_lambda(lq_ref, lk_ref, lam_init):
    lq = lq_ref[...]
    lk = lk_ref[...]
    dots = jnp.sum(lq * lk, -1, keepdims=True)
    return jnp.exp(dots[0:1, :]) - jnp.exp(dots[1:2, :]) + lam_init


def _attn_kernel(tq, lam_init, qi_tab, ki_tab, qt_ref, k_ref, vt_ref, lq_ref, lk_ref, g_ref, o_ref,
                 qs_ref, m_ref, l_ref, acc_ref):
    t = pl.program_id(2)
    qi = qi_tab[t]
    ki = ki_tab[t]

    @pl.when(ki == 0)
    def _():
        qt = qt_ref[0]
        row = lax.broadcasted_iota(jnp.int32, qt.shape, 0)
        qs_ref[:, 0:tq] = jnp.where(row < B_HEAD, qt, jnp.zeros_like(qt))
        qs_ref[:, tq:2 * tq] = jnp.where(row >= B_HEAD, qt, jnp.zeros_like(qt))
        m_ref[...] = jnp.full(m_ref.shape, NEG, F32)
        l_ref[...] = jnp.zeros_like(l_ref)
        acc_ref[...] = jnp.zeros_like(acc_ref)

    def update(diagonal):
        s = _dot(k_ref[0], qs_ref[...])
        if diagonal:
            kpos = lax.broadcasted_iota(jnp.int32, s.shape, 0)
            qpos = lax.broadcasted_iota(jnp.int32, s.shape, 1) & (tq - 1)
            s = jnp.where(kpos <= qpos, s, NEG)
        m_prev = m_ref[...]
        m_new = jnp.maximum(m_prev, jnp.max(s, 0, keepdims=True))
        alpha = jnp.exp2(m_prev - m_new)
        p = jnp.exp2(s - m_new)
        l_ref[...] = alpha * l_ref[...] + jnp.sum(p, 0, keepdims=True)
        acc_ref[...] = alpha * acc_ref[...] + _dot(vt_ref[0], p.astype(BF16))
        m_ref[...] = m_new

    @pl.when(ki < qi)
    def _():
        update(False)

    @pl.when(ki == qi)
    def _():
        update(True)
        o_all = acc_ref[...] * (1.0 / l_ref[...])
        lam = _diff_lambda(lq_ref, lk_ref, lam_init)
        o = o_all[:, 0:tq] - lam * o_all[:, tq:2 * tq]
        on = o * lax.rsqrt(jnp.mean(o * o, 0, keepdims=True) + SUBLN_EPS) * g_ref[...] * (1.0 - lam_init)
        o_ref[0] = on.T.astype(o_ref.dtype)


def _diff_attn_prompt(qt, k, vt, lam_q, lam_k, g_col, lam_init, tq):
    b, s, d = k.shape
    heads = d // LANES
    nq = s // tq
    pairs = [(qi, ki) for qi in range(nq) for ki in range(qi + 1)]
    qi_tab = jnp.asarray([p[0] for p in pairs], jnp.int32)
    ki_tab = jnp.asarray([p[1] for p in pairs], jnp.int32)
    qt_spec = pl.BlockSpec((1, LANES, tq), lambda bi, h, t, qt_, kt_: (bi, h, qt_[t]))
    vt_spec = pl.BlockSpec((1, LANES, tq), lambda bi, h, t, qt_, kt_: (bi, h, kt_[t]))
    k_spec = pl.BlockSpec((1, tq, LANES), lambda bi, h, t, qt_, kt_: (bi, kt_[t], h))
    o_spec = pl.BlockSpec((1, tq, LANES), lambda bi, h, t, qt_, kt_: (bi, qt_[t], h))
    small = lambda a: pl.BlockSpec(a.shape, lambda *_: (0,) * a.ndim)
    grid_spec = pltpu.PrefetchScalarGridSpec(
        num_scalar_prefetch=2,
        grid=(b, heads, len(pairs)),
        in_specs=[qt_spec, k_spec, vt_spec, small(lam_q), small(lam_k), small(g_col)],
        out_specs=o_spec,
        scratch_shapes=[pltpu.VMEM((LANES, 2 * tq), BF16), pltpu.VMEM((1, 2 * tq), F32),
                        pltpu.VMEM((1, 2 * tq), F32), pltpu.VMEM((LANES, 2 * tq), F32)],
    )
    return pl.pallas_call(
        functools.partial(_attn_kernel, tq, lam_init),
        out_shape=jax.ShapeDtypeStruct((b, s, d), BF16),
        grid_spec=grid_spec,
        compiler_params=_params(("arbitrary",) * 3),
        name="diff_attn_prompt",
    )(qi_tab, ki_tab, qt, k, vt, lam_q, lam_k, g_col)


def _decode_kernel(pp, lam_init, heads, pt_ref, qcol_ref, qrow_ref, kn_ref, vn_ref, lq_ref, lk_ref, g_ref, *rest):
    k_refs = rest[:pp]
    v_refs = rest[pp:2 * pp]
    o_ref, qb_ref, m_ref, l_ref, acc_ref = rest[2 * pp:]
    step = pl.program_id(1)
    rows = 2 * heads
    d = qb_ref.shape[0]

    @pl.when(step == 0)
    def _():
        qb_ref[...] = jnp.broadcast_to(qcol_ref[0], qb_ref.shape)
        m_ref[...] = jnp.full(m_ref.shape, NEG, F32)
        l_ref[...] = jnp.zeros_like(l_ref)
        acc_ref[...] = jnp.zeros_like(acc_ref)

    r = lax.broadcasted_iota(jnp.int32, (rows, d), 0)
    c = lax.broadcasted_iota(jnp.int32, (rows, d), 1)
    sel = ((c >> 7) == (r & (heads - 1))) & (((c >> 6) & 1) == (r // heads))
    sel_b = jnp.where(sel, 1.0, 0.0).astype(BF16)
    row_head = lax.broadcasted_iota(jnp.int32, (rows, LANES), 0) & (heads - 1)

    s = jnp.concatenate([_dot(sel_b, (k_refs[i][0] * qb_ref[...]).astype(BF16)) for i in range(pp)], axis=1)
    m_prev = m_ref[...]
    m_new = jnp.maximum(m_prev, jnp.max(s, -1, keepdims=True))
    alpha = jnp.exp(m_prev - m_new)
    p = jnp.exp(s - m_new)
    l_ref[...] = alpha * l_ref[...] + jnp.sum(p, -1, keepdims=True)
    pb = p.astype(BF16)
    pv = jnp.zeros((rows, LANES), F32)
    for i in range(pp):
        pb_i = pb[:, i * PAGE_SIZE:(i + 1) * PAGE_SIZE]
        for h in range(heads):
            v_h = v_refs[i][0, pl.ds(h, PAGE_SIZE, stride=heads), :].astype(BF16)
            pv = pv + _dot(jnp.where(row_head == h, pb_i, jnp.zeros_like(pb_i)), v_h)
    acc_ref[...] = alpha * acc_ref[...] + pv
    m_ref[...] = m_new

    @pl.when(step == pl.num_programs(1) - 1)
    def _():
        s_new = jnp.sum(jnp.where(sel, qrow_ref[0] * kn_ref[0], 0.0), -1, keepdims=True)
        m_prev = m_ref[...]
        m_fin = jnp.maximum(m_prev, s_new)
        alpha = jnp.exp(m_prev - m_fin)
        p_new = jnp.exp(s_new - m_fin)
        l_fin = alpha * l_ref[...] + p_new
        o_all = (alpha * acc_ref[...] + p_new * vn_ref[0]) / l_fin
        lam = _diff_lambda(lq_ref, lk_ref, lam_init)
        o = o_all[0:heads] - lam * o_all[heads:rows]
        on = o * lax.rsqrt(jnp.mean(o * o, -1, keepdims=True) + SUBLN_EPS) * g_ref[...] * (1.0 - lam_init)
        o_ref[0] = on


def _diff_attn_decode(q, k_new, v_new, cache_kt, cache_v, page_table, lam_q, lam_k, g_row, lam_init, pp):
    bd, d = q.shape
    heads = d // LANES
    rows = 2 * heads
    n_pages = page_table.shape[1]
    at_b = lambda shape: pl.BlockSpec((1,) + shape, lambda b, s, pt: (b, 0, 0))

    def page_spec(shape, i):
        return pl.BlockSpec((1,) + shape, lambda b, s, pt: (pt[b, s * pp + i], 0, 0))

    small = lambda a: pl.BlockSpec(a.shape, lambda b, s, pt: (0,) * a.ndim)
    grid_spec = pltpu.PrefetchScalarGridSpec(
        num_scalar_prefetch=1,
        grid=(bd, n_pages // pp),
        in_specs=[at_b((d, 1)), at_b((1, d)), at_b((1, d)), at_b((rows, LANES)), small(lam_q), small(lam_k), small(g_row)]
                 + [page_spec((d, PAGE_SIZE), i) for i in range(pp)]
                 + [page_spec((PAGE_SIZE * heads, LANES), i) for i in range(pp)],
        out_specs=at_b((heads, LANES)),
        scratch_shapes=[pltpu.VMEM((d, PAGE_SIZE), F32), pltpu.VMEM((rows, 1), F32), pltpu.VMEM((rows, 1), F32),
                        pltpu.VMEM((rows, LANES), F32)],
    )
    v_rows = jnp.tile(v_new, (1, 2, 1))
    return pl.pallas_call(
        functools.partial(_decode_kernel, pp, lam_init, heads),
        out_shape=jax.ShapeDtypeStruct((bd, heads, LANES), F32),
        grid_spec=grid_spec,
        compiler_params=_params(("arbitrary", "arbitrary")),
        name="diff_attn_decode",
    )(page_table, q.reshape(bd, d, 1), q.reshape(bd, 1, d), k_new.reshape(bd, 1, d), v_rows, lam_q, lam_k, g_row,
      *([cache_kt] * pp), *([cache_v] * pp))


def _pad_to(a, axis, size):
    pad = [(0, 0)] * a.ndim
    pad[axis] = (0, size - a.shape[axis])
    return jnp.pad(a, pad)


def _pack_state(s):
    b, h, n, _ = s.shape
    return s.reshape(b, h // 2, 2, n, n).transpose(0, 1, 3, 2, 4).reshape(b, h // 2, n, 2 * n)


def _unpack_state(s):
    b, p, n, _ = s.shape
    return s.reshape(b, p, n, 2, n).transpose(0, 1, 3, 2, 4).reshape(b, 2 * p, n, n)


def kernel(x_prompt, x_sample, cache_k, cache_v, state_wkv, state_shift, page_table, c_prompt, c_sample, ln_g, ln_b, ada_w, ada_b, ffn_w1, ffn_w2, a_mu, a_w_rkv, a_w_o, a_w0, a_w1, a_w2, a_a0, a_a1, a_a2, a_g1, a_g2, a_k_k, a_k_a, a_r_k, a_gn_g, a_gn_b, kv_ada_w, kv_ada_b, kv_w_k, kv_w_v, b_w_q, b_w_o, b_lam_q, b_lam_k, b_subln_g):
    bp, seq, d = x_prompt.shape
    bd, dec_seq, _ = x_sample.shape
    depth = ada_w.shape[0]
    n_a = a_mu.shape[0]
    assert depth == 2 and n_a == 1 and dec_seq == 1
    alpha = (2 * depth) ** 0.25
    heads = d // LANES
    pool = cache_k.shape[0]

    n_c = bp + bd
    c_all = _pad_to(jnp.concatenate([c_prompt, c_sample], axis=0), 0, -(-n_c // 8) * 8)
    mods = [_ada(c_all, ada_w[l], ada_b[l]) for l in range(depth)]
    kvmods = _ada(c_all, kv_ada_w, kv_ada_b)

    wrkv = a_w_rkv[0].astype(BF16)
    lora = 128
    w1 = _pad_to(a_w1[0], 1, lora).astype(BF16)
    w2 = _pad_to(a_w2[0], 0, lora).astype(BF16)
    a1 = _pad_to(a_a1[0], 1, lora).astype(BF16)
    a2 = _pad_to(a_a2[0], 0, lora).astype(BF16)
    g1 = _pad_to(a_g1[0], 1, 2 * lora).astype(BF16)
    g2 = _pad_to(a_g2[0], 0, 2 * lora).astype(BF16)
    vec = _pad_to(jnp.stack([a_w0[0], a_a0[0], a_k_k[0], a_k_a[0], a_r_k[0].reshape(d)]), 0, 8)
    gn = jnp.stack([a_gn_g[0], a_gn_b[0]])
    a_wo = a_w_o[0].astype(BF16)
    ffn1 = ffn_w1.astype(BF16)
    ffn2 = ffn_w2.astype(BF16)
    wk = kv_w_k.astype(BF16)
    wv = kv_w_v.astype(BF16)
    wq = b_w_q[0].astype(BF16)
    b_wo = b_w_o[0].astype(BF16)
    g_head = b_subln_g[0].reshape(1, LANES)
    lam_init = 0.8 - 0.6 * math.exp(-0.3 * n_a)
    ln = [[jnp.stack([ln_g[l, s], ln_b[l, s]]) for s in range(2)] for l in range(depth)]
    attn_scale = B_HEAD ** -0.5

    def trunk(x, rows, shift0, wkv0, seq_len, tm, tm_mlp, tc, attend):
        b = x.shape[0]
        x2d = x.reshape(b * seq_len, d)
        tiles = max(seq_len // tm, 1)
        tiles_mlp = max(seq_len // tm_mlp, 1)
        if seq_len == 1:
            shape_mod = lambda m: m[rows].reshape(1, b, -1)
            sh0 = shift0.reshape(1, b, d)
        else:
            shape_mod = lambda m: m[rows].reshape(b, 1, -1)
            sh0 = shift0.reshape(b, 1, d)
        mod0, mod1, kvmod = shape_mod(mods[0]), shape_mod(mods[1]), shape_mod(kvmods)

        *seqs, g, bonus, last = _rwkv_proj(x2d, mod0, sh0, seq_len, tm, a_mu[0], wrkv, w1, w2, a1, a2, g1, g2, vec)
        seqs = [s.reshape(b, seq_len, d) for s in seqs]
        if seq_len == 1:
            y, s_fin = _wkv(seqs, _pack_state(wkv0), tc, 2)
            s_fin = _unpack_state(s_fin)
        else:
            y, s_fin = _wkv_chunked(seqs, _pair_block_diag(wkv0), 2)
            s_fin = _pair_diag_blocks(s_fin)
        x1 = _mix_out(True, alpha, [y.reshape(b * seq_len, d), g, bonus], x2d, mod0, tiles, tm, a_wo, gn, ln[0][0])
        x2 = _mlp(alpha, x1, mod0, tiles_mlp, tm_mlp, ffn1[0], ffn2[0], ln[0][1])
        if seq_len > 1:
            kt, v, kb, vt, qt = _qkv(x2, kvmod, mod1, tiles, tm, wk, wv, wq, attn_scale * LOG2E, True)
            k_out = kt.reshape(b, heads, 2, B_HEAD, seq_len).transpose(0, 4, 1, 2, 3)
            on = attend(kb, vt, qt)
        else:
            k, v, q = _qkv(x2, kvmod, mod1, tiles, tm, wk, wv, wq, attn_scale, False)
            k_out = k.reshape(b, seq_len, heads, 2, B_HEAD)
            on = attend(k, v, q)
        x3 = _mix_out(False, alpha, [on], x2, mod1, tiles, tm, b_wo, gn, ln[1][0])
        y_out = _mlp(alpha, x3, mod1, tiles_mlp, tm_mlp, ffn1[1], ffn2[1], ln[1][1])
        return (y_out.reshape(b, seq_len, d), k_out, v.reshape(b, seq_len, heads, 2 * B_HEAD),
                s_fin[None], last.reshape(1, b, d))

    def attend_prompt(kb, vt, qt):
        on = _diff_attn_prompt(qt, kb.reshape(bp, seq, d), vt, b_lam_q[0], b_lam_k[0], g_head.reshape(LANES, 1),
                               lam_init, min(1024, seq))
        return on.reshape(bp * seq, d)

    def attend_sample(k, v, q):
        cache_kt = jnp.transpose(cache_k, (0, 2, 3, 4, 1)).reshape(pool, d, PAGE_SIZE)
        on = _diff_attn_decode(q, k, v.reshape(bd, heads, LANES), cache_kt,
                               cache_v.reshape(pool, PAGE_SIZE * heads, LANES), page_table,
                               b_lam_q[0], b_lam_k[0], g_head, lam_init, min(8, page_table.shape[1]))
        return on.reshape(bd, d)

    zeros_shift = jnp.zeros((bp, d), F32)
    zeros_wkv = jnp.zeros((bp, d // A_HEAD, A_HEAD, A_HEAD), F32)
    y_p, k_p, v_p, wkv_p, shift_p = trunk(x_prompt, slice(0, bp), zeros_shift, zeros_wkv, seq,
                                          min(256, seq), min(512, seq), min(128, seq), attend_prompt)
    y_s, k_s, v_s, wkv_s, shift_s = trunk(x_sample, slice(bp, bp + bd), state_shift[0], state_wkv[0], 1,
                                          bd, bd, 1, attend_sample)
    return (y_p, y_s, k_p, v_p, wkv_p, shift_p, k_s, v_s, wkv_s, shift_s)
```

```python
import functools
import math

import jax
import jax.numpy as jnp
from jax import lax
from jax.experimental import pallas as pl
from jax.experimental.pallas import tpu as pltpu

F32 = jnp.float32
BF16 = jnp.bfloat16

LANES = 128
A_HEAD = 64
B_HEAD = 64
PAGE_SIZE = 128
GN_EPS = 64e-5
SUBLN_EPS = 1e-5
LN_EPS = 1e-5
NEG = -1e30
LOG2E = math.log2(math.e)
VMEM_LIMIT = 56 * 2**20


def _params(sem):
    return pltpu.CompilerParams(dimension_semantics=sem, vmem_limit_bytes=VMEM_LIMIT)


def _dot(a, b):
    return jnp.dot(a, b, preferred_element_type=F32)


def _split(x):
    hi = x.astype(BF16)
    lo = (x - hi.astype(F32)).astype(BF16)
    return hi, lo


def _sigmoid(z):
    return 1.0 / (1.0 + jnp.exp(-z))


def _layer_norm(z, g, b):
    mu = jnp.mean(z, -1, keepdims=True)
    d = z - mu
    var = jnp.mean(d * d, -1, keepdims=True)
    return d * lax.rsqrt(var + LN_EPS) * g + b


def _pair_ones():
    r = lax.broadcasted_iota(jnp.int32, (LANES, LANES), 0) >> 6
    c = lax.broadcasted_iota(jnp.int32, (LANES, LANES), 1) >> 6
    return jnp.where(r == c, 1.0, 0.0).astype(BF16)


def _head_sum(x, ones):
    m, d = x.shape
    cols = d // LANES
    stacked = jnp.concatenate([x[:, c * LANES:(c + 1) * LANES] for c in range(cols)], axis=0)
    hi, lo = _split(stacked)
    s = _dot(hi, ones) + _dot(lo, ones)
    return jnp.concatenate([s[c * m:(c + 1) * m] for c in range(cols)], axis=1)


def _mod_spec(mod, tiles_per_batch):
    return pl.BlockSpec((1, mod.shape[1], mod.shape[2]), lambda i, *_: (i // tiles_per_batch, 0, 0))


def _full(a):
    nd = a.ndim
    return pl.BlockSpec(a.shape, lambda *_: (0,) * nd)


def _ada_kernel(c_ref, w_ref, b_ref, o_ref):
    c = c_ref[...]
    s = c * _sigmoid(c)
    o_ref[...] = _dot(s.astype(BF16), w_ref[0].astype(BF16)) + b_ref[0]


def _ada(c, w, b, layer):
    m, d = c.shape
    n = w.shape[2]
    tn = 1024
    return pl.pallas_call(
        _ada_kernel,
        out_shape=jax.ShapeDtypeStruct((m, n), F32),
        grid=(n // tn,),
        in_specs=[pl.BlockSpec((m, d), lambda j: (0, 0)),
                  pl.BlockSpec((1, d, tn), lambda j: (layer, 0, j)),
                  pl.BlockSpec((1, 1, tn), lambda j: (layer, 0, j))],
        out_specs=pl.BlockSpec((m, tn), lambda j: (0, j)),
        compiler_params=_params(("arbitrary",)),
        name="ada_proj",
    )(c, w, b)


def _rwkv_proj_kernel(single_step, tiles_per_batch, d,
                      x_ref, xp_ref, mod_ref, sh0_ref, mu_ref, wrkv_ref, w1_ref, w2_ref, a1_ref, a2_ref,
                      g1_ref, g2_ref, vec_ref,
                      r_o, w_o, k_o, v_o, a_o, b_o, g_o, bonus_o, last_o):
    mod = mod_ref[0]
    sh = mod[:, 0:d]
    sc = mod[:, d:2 * d]
    x = x_ref[...]
    h = x * (1.0 + sc) + sh
    if single_step:
        h_prev = sh0_ref[0]
        last_o[0] = h
    else:
        i = pl.program_id(0)
        prev_row = xp_ref[7:8, :] * (1.0 + sc) + sh
        first = jnp.where(i % tiles_per_batch == 0, sh0_ref[0], prev_row)
        rolled = pltpu.roll(h, 1, 0)
        rid = lax.broadcasted_iota(jnp.int32, h.shape, 0)
        h_prev = jnp.where(rid == 0, first, rolled)
        last_o[0] = h[h.shape[0] - 1:, :]
    xx = h_prev - h

    def mix(p):
        return (h + xx * mu_ref[p:p + 1, :]).astype(BF16)

    r = _dot(mix(0), wrkv_ref[0])
    k = _dot(mix(1), wrkv_ref[1])
    v = _dot(mix(2), wrkv_ref[2])
    lw = _dot(jnp.tanh(_dot(mix(3), w1_ref[...])).astype(BF16), w2_ref[...])
    la = _dot(_dot(mix(4), a1_ref[...]).astype(BF16), a2_ref[...])
    g = _dot(_sigmoid(_dot(mix(5), g1_ref[...])).astype(BF16), g2_ref[...])

    w0 = vec_ref[0:1, :]
    a0 = vec_ref[1:2, :]
    k_k = vec_ref[2:3, :]
    k_a = vec_ref[3:4, :]
    r_k = vec_ref[4:5, :]

    z = -(w0 + lw)
    softplus = jnp.maximum(z, 0.0) + jnp.log(1.0 + jnp.exp(-jnp.abs(z)))
    w_log = -softplus - 0.5
    log_decay = -jnp.exp(w_log)
    a_gate = _sigmoid(a0 + la)
    ones = _pair_ones()
    kkf = k * k_k
    norm = jnp.sqrt(_head_sum(kkf * kkf, ones))
    kk = kkf / jnp.maximum(norm, 1e-12)
    k_mod = k * (1.0 + (a_gate - 1.0) * k_a)
    bonus = _head_sum(r * k_mod * r_k, ones) * v

    w_o[...] = jnp.exp(log_decay) if single_step else log_decay
    for ref, val in ((r_o, r), (k_o, k_mod), (v_o, v), (a_o, -kk), (b_o, kk * a_gate), (g_o, g), (bonus_o, bonus)):
        ref[...] = val.astype(ref.dtype)


def _rwkv_proj(x2d, mod, shift0, seq_len, tm, mu, wrkv, w1, w2, a1, a2, g1, g2, vec):
    n, d = x2d.shape
    single = seq_len == 1
    tiles_per_batch = 1 if single else seq_len // tm
    nt = n // tm
    nb = mod.shape[0]
    last_rows = tm if single else 1
    row = pl.BlockSpec((tm, d), lambda i: (i, 0))
    if single:
        xp_spec = pl.BlockSpec((8, d), lambda i: (0, 0))
        sh0_spec = pl.BlockSpec((1, tm, d), lambda i: (i, 0, 0))
    else:
        xp_spec = pl.BlockSpec((8, d), lambda i: (jnp.maximum(i * (tm // 8) - 1, 0), 0))
        sh0_spec = pl.BlockSpec((1, 1, d), lambda i: (i // tiles_per_batch, 0, 0))
    sds = lambda dtype: jax.ShapeDtypeStruct((n, d), dtype)
    seq_dtype = F32 if single else BF16
    outs = pl.pallas_call(
        functools.partial(_rwkv_proj_kernel, single, tiles_per_batch, d),
        out_shape=[sds(seq_dtype), sds(F32)] + [sds(seq_dtype)] * 6 + [jax.ShapeDtypeStruct((nb, last_rows, d), F32)],
        grid=(nt,),
        in_specs=[row, xp_spec, _mod_spec(mod, tiles_per_batch), sh0_spec, _full(mu), _full(wrkv), _full(w1),
                  _full(w2), _full(a1), _full(a2), _full(g1), _full(g2), _full(vec)],
        out_specs=[row] * 8 + [pl.BlockSpec((1, last_rows, d), lambda i: (i // tiles_per_batch, 0, 0))],
        compiler_params=_params(("arbitrary",)),
        name="rwkv_proj",
    )(x2d, x2d, mod, shift0, mu, wrkv, w1, w2, a1, a2, g1, g2, vec)
    return outs


def _wkv_kernel(tc, r_ref, w_ref, k_ref, v_ref, a_ref, b_ref, s0_ref, y_ref, s_ref):
    @pl.when(pl.program_id(1) == 0)
    def _():
        s_ref[...] = s0_ref[...]

    groups, pairs = s_ref.shape[0], s_ref.shape[1]
    ones = _pair_ones()
    rid = lax.broadcasted_iota(jnp.int32, (A_HEAD, LANES), 0)
    cid = lax.broadcasted_iota(jnp.int32, (A_HEAD, LANES), 1) & (A_HEAD - 1)
    diag = rid == cid
    idx = [(g, p) for g in range(groups) for p in range(pairs)]
    n = len(idx)
    sets = [list(range(n))]

    rb = min(8, tc)
    sub = lax.broadcasted_iota(jnp.int32, (rb, LANES), 0)

    def block(tb, carry):
        base = pl.multiple_of(tb * rb, rb)

        def tile(ref, g, p):
            return ref[g, pl.ds(base, rb), p * LANES:(p + 1) * LANES]

        r_t, w_t, k_t, v_t, a_t, b_t = ([tile(ref, g, p) for g, p in idx]
                                        for ref in (r_ref, w_ref, k_ref, v_ref, a_ref, b_ref))
        states = [s_ref[g, p] for g, p in idx]
        y_tiles = [jnp.zeros((rb, LANES), F32) for _ in idx]
        for i in range(rb):
            row = lambda tiles, j: tiles[j][i:i + 1, :]
            for members in sets:
                sa_in = jnp.concatenate([(states[j] * row(a_t, j)).astype(BF16) for j in members], axis=0)
                vd = jnp.concatenate([jnp.where(diag, row(v_t, j), 0.0).astype(BF16) for j in members], axis=0)
                sa = _dot(sa_in, ones)
                vb = _dot(vd, ones)
                for q, j in enumerate(members):
                    sl = slice(q * A_HEAD, (q + 1) * A_HEAD)
                    states[j] = states[j] * row(w_t, j) + sa[sl] * row(b_t, j) + vb[sl] * row(k_t, j)
                y_in = jnp.concatenate([(states[j] * row(r_t, j)).astype(BF16) for j in members], axis=0)
                yb = _dot(y_in, ones)
                for q, j in enumerate(members):
                    sl = slice(q * A_HEAD, (q + 1) * A_HEAD)
                    y_row = jnp.sum(jnp.where(diag, yb[sl], 0.0), axis=0, keepdims=True)
                    y_tiles[j] = jnp.where(sub == i, y_row, y_tiles[j])
        for j, (g, p) in enumerate(idx):
            s_ref[g, p] = states[j]
            y_ref[g, pl.ds(base, rb), p * LANES:(p + 1) * LANES] = y_tiles[j]
        return carry

    lax.fori_loop(0, tc // rb, block, 0)


def _wkv(seqs, s0, tc, group):
    b, t, d = seqs[0].shape
    pairs = s0.shape[1]
    seq_spec = pl.BlockSpec((group, tc, d), lambda i, j: (i, j, 0))
    st_spec = pl.BlockSpec((group, pairs, A_HEAD, LANES), lambda i, j: (i, 0, 0, 0))
    return pl.pallas_call(
        functools.partial(_wkv_kernel, tc),
        out_shape=[jax.ShapeDtypeStruct((b, t, d), F32), jax.ShapeDtypeStruct(s0.shape, F32)],
        grid=(b // group, t // tc),
        in_specs=[seq_spec] * 6 + [st_spec],
        out_specs=[seq_spec, st_spec],
        compiler_params=_params(("arbitrary", "arbitrary")),
        name="wkv7_scan",
    )(*seqs, s0)


WKV_CHUNK = 64


def _dot_nt(a, b):
    return lax.dot_general(a, b, (((1,), (1,)), ((), ())), preferred_element_type=F32)


def _dot_tn(a, b):
    return lax.dot_general(a, b, (((0,), (0,)), ((), ())), preferred_element_type=F32)


def _mm3(a, b):
    ah, al = _split(a)
    bh, bl = _split(b)
    return _dot(ah, bh) + _dot(ah, bl) + _dot(al, bh)


def _wkv_chunk_kernel(r_ref, lw_ref, k_ref, v_ref, a_ref, b_ref, s0_ref, y_ref, s_ref):
    @pl.when(pl.program_id(1) == 0)
    def _():
        s_ref[...] = s0_ref[...]

    groups, pairs = s_ref.shape[0], s_ref.shape[1]
    c = r_ref.shape[1]
    iota = lambda shape, dim: lax.broadcasted_iota(jnp.int32, shape, dim)
    head0 = iota((c, LANES), 1) < A_HEAD
    tri = jnp.where(iota((c, c), 1) <= iota((c, c), 0), 1.0, 0.0).astype(BF16)
    rows4 = iota((4 * c, LANES), 0)
    t_idx = rows4 & (c - 1)
    s_idx = iota((4 * c, LANES), 1) & (c - 1)
    keep = s_idx < t_idx + jnp.where(rows4 < 2 * c, 0, 1)
    same_head = (iota((LANES, LANES), 0) >> 6) == (iota((LANES, LANES), 1) >> 6)
    same_block = (iota((LANES, LANES), 0) >> 3) == (iota((LANES, LANES), 1) >> 3)
    eye = jnp.where(iota((LANES, LANES), 0) == iota((LANES, LANES), 1), 1.0, 0.0)

    def by_head(x):
        return jnp.concatenate([jnp.where(head0, x, 0.0), jnp.where(head0, 0.0, x)], axis=0)

    units = [(g, p) for g in range(groups) for p in range(pairs)]
    each = lambda f, *lists: [f(*args) for args in zip(*lists)]
    bf = lambda xs: [x.astype(BF16) for x in xs]
    swap = lambda x: pltpu.roll(x, A_HEAD, 1)

    def load(ref):
        return [ref[g, :, p * LANES:(p + 1) * LANES].astype(F32) for g, p in units]

    r, lw, k, v, a, b = (load(ref) for ref in (r_ref, lw_ref, k_ref, v_ref, a_ref, b_ref))
    s_old = [s_ref[g, p] for g, p in units]

    def cumulative(x):
        hi, lo = _split(x)
        return _dot(tri, hi) + _dot(tri, lo)

    cum = each(cumulative, lw)
    end = [x[c - 1:c, :] for x in cum]
    inv = [jnp.exp(-x) for x in cum]
    rest = each(lambda e, x: jnp.exp(e - x), end, cum)
    at = each(lambda a_, x, l_: a_ * jnp.exp(x - l_), a, cum, lw)
    rt = each(lambda r_, x: r_ * jnp.exp(x), r, cum)
    bt = each(jnp.multiply, b, inv)
    kt = each(jnp.multiply, k, inv)
    bb = bf(each(jnp.multiply, b, rest))
    kb = bf(each(jnp.multiply, k, rest))
    at_b, rt_b, v_b = bf(at), bf(rt), bf(v)
    lhs = each(lambda x, y: jnp.concatenate([by_head(x), by_head(y)], axis=0).astype(BF16), at, rt)
    rhs = each(lambda x, y: jnp.concatenate([x, y], axis=0).astype(BF16), bt, kt)
    aa = each(lambda x, y: jnp.where(keep, _dot_nt(x, y), 0.0), lhs, rhs)
    a0, a1, r0, r1 = ([x[i * c:(i + 1) * c] for x in aa] for i in range(4))
    n = each(lambda x, y: jnp.concatenate([jnp.where(head0, x, 0.0), jnp.where(head0, 0.0, swap(y))], axis=0), a0, a1)
    a_ak = bf(each(lambda x, y: jnp.where(head0, swap(x), y), a0, a1))
    a_rb = bf(each(lambda x, y: jnp.where(head0, x, swap(y)), r0, r1))
    a_rk = bf(each(lambda x, y: jnp.where(head0, swap(x), y), r0, r1))
    def neumann8(z):
        out = [eye + z_ for z_ in z]
        for _ in range(2):
            z = each(_mm3, z, z)
            out = each(lambda o_, z_: o_ + _mm3(o_, z_), out, z)
        return out

    d_in = [jnp.where(same_block, n_, 0.0) for n_ in n]
    d_inv = neumann8(d_in)
    m_mat = each(lambda di_, n_, d_: _mm3(di_, n_ - d_), d_inv, n, d_in)
    inv_t = each(_mm3, neumann8(m_mat), d_inv)
    v_heads = bf(each(by_head, v))
    s_b = bf(s_old)
    rhs0 = each(lambda at_, s_, ak_, vh_: _dot_nt(at_, s_) + _dot(ak_, vh_), at_b, s_b, a_ak, v_heads)
    uu = each(lambda t_, x_: _dot(t_.astype(BF16), by_head(x_).astype(BF16)), inv_t, rhs0)
    u = [x_[0:c] + x_[c:2 * c] for x_ in uu]
    u_heads = bf(each(by_head, u))
    y = each(lambda rt_, s_, rb_, uh_, rk_, vh_: _dot_nt(rt_, s_) + _dot(rb_, uh_) + _dot(rk_, vh_),
             rt_b, s_b, a_rb, u_heads, a_rk, v_heads)
    upd = each(lambda u_, bb_, v_, kb_: _dot_tn(u_, bb_) + _dot_tn(v_, kb_), bf(u), bb, v_b, kb)
    s_new = each(lambda s_, e, up: s_ * jnp.exp(e) + jnp.where(same_head, up, 0.0), s_old, end, upd)
    for (g, p), y_, s_ in zip(units, y, s_new):
        y_ref[g, :, p * LANES:(p + 1) * LANES] = y_
        s_ref[g, p] = s_


def _wkv_chunked(seqs, s0, group):
    b, t, d = seqs[0].shape
    pairs = s0.shape[1]
    seq_spec = pl.BlockSpec((group, WKV_CHUNK, d), lambda i, j: (i, j, 0))
    st_spec = pl.BlockSpec((group, pairs, LANES, LANES), lambda i, j: (i, 0, 0, 0))
    return pl.pallas_call(
        _wkv_chunk_kernel,
        out_shape=[jax.ShapeDtypeStruct((b, t, d), F32), jax.ShapeDtypeStruct(s0.shape, F32)],
        grid=(b // group, t // WKV_CHUNK),
        in_specs=[seq_spec] * 6 + [st_spec],
        out_specs=[seq_spec, st_spec],
        compiler_params=_params(("arbitrary", "arbitrary")),
        name="wkv7_chunked",
    )(*seqs, s0)


def _pair_block_diag(s):
    b, h, n, _ = s.shape
    s = s.reshape(b, h // 2, 2, n, n)
    z = jnp.zeros_like(s[:, :, 0])
    top = jnp.concatenate([s[:, :, 0], z], axis=-1)
    bot = jnp.concatenate([z, s[:, :, 1]], axis=-1)
    return jnp.concatenate([top, bot], axis=-2)


def _pair_diag_blocks(s):
    b, p, _, _ = s.shape
    n = A_HEAD
    return jnp.stack([s[:, :, 0:n, 0:n], s[:, :, n:, n:]], axis=2).reshape(b, 2 * p, n, n)


def _mix_out_kernel(rwkv, alpha, d, *refs):
    if rwkv:
        y_ref, g_ref, bonus_ref, x_ref, mod_ref, wo_ref, gn_ref, ln_ref, o_ref = refs
        y = y_ref[...]
        ones = _pair_ones()
        mu_y = _head_sum(y, ones) * (1.0 / A_HEAD)
        dy = y - mu_y
        var_y = _head_sum(dy * dy, ones) * (1.0 / A_HEAD)
        yn = dy * lax.rsqrt(var_y + GN_EPS) * gn_ref[0:1, :] + gn_ref[1:2, :]
        pre = ((yn + bonus_ref[...].astype(F32)) * g_ref[...].astype(F32)).astype(BF16)
    else:
        on_ref, x_ref, mod_ref, wo_ref, ln_ref, o_ref = refs
        pre = on_ref[...].astype(BF16)
    mix = _dot(pre, wo_ref[...])
    gt = mod_ref[0][:, 2 * d:3 * d]
    z = alpha * x_ref[...] + (1.0 + gt) * mix
    o_ref[...] = _layer_norm(z, ln_ref[0:1, :], ln_ref[1:2, :])


def _mix_out(rwkv, alpha, ins, x2d, mod, tiles_per_batch, tm, wo, gn, ln):
    n, d = x2d.shape
    row = pl.BlockSpec((tm, d), lambda i: (i, 0))
    extra = [gn] if rwkv else []
    args = list(ins) + [x2d, mod, wo] + extra + [ln]
    specs = [row] * (len(ins) + 1) + [_mod_spec(mod, tiles_per_batch), _full(wo)] + [_full(e) for e in extra] + [_full(ln)]
    return pl.pallas_call(
        functools.partial(_mix_out_kernel, rwkv, alpha, d),
        out_shape=jax.ShapeDtypeStruct((n, d), F32),
        grid=(n // tm,),
        in_specs=specs,
        out_specs=row,
        compiler_params=_params(("arbitrary",)),
        name="rwkv_out_ln" if rwkv else "attn_out_ln",
    )(*args)


def _mlp_kernel(alpha, d, x_ref, mod_ref, w1_ref, w2_ref, ln_ref, o_ref, hf_ref, acc_ref):
    j = pl.program_id(1)
    mod = mod_ref[0]

    @pl.when(j == 0)
    def _():
        hf_ref[...] = (x_ref[...] * (1.0 + mod[:, 4 * d:5 * d]) + mod[:, 3 * d:4 * d]).astype(BF16)
        acc_ref[...] = jnp.zeros_like(acc_ref)

    h1 = jnp.maximum(_dot(hf_ref[...], w1_ref[...]), 0.0)
    acc_ref[...] += _dot((h1 * h1).astype(BF16), w2_ref[...])

    @pl.when(j == pl.num_programs(1) - 1)
    def _():
        z = alpha * x_ref[...] + (1.0 + mod[:, 5 * d:6 * d]) * acc_ref[...]
        o_ref[...] = _layer_norm(z, ln_ref[0:1, :], ln_ref[1:2, :])


def _mlp(alpha, x2d, mod, tiles_per_batch, tm, w1, w2, ln):
    n, d = x2d.shape
    dff = w1.shape[1]
    tf = 1024
    row = pl.BlockSpec((tm, d), lambda i, j: (i, 0))
    return pl.pallas_call(
        functools.partial(_mlp_kernel, alpha, d),
        out_shape=jax.ShapeDtypeStruct((n, d), F32),
        grid=(n // tm, dff // tf),
        in_specs=[row, _mod_spec(mod, tiles_per_batch),
                  pl.BlockSpec((d, tf), lambda i, j: (0, j)),
                  pl.BlockSpec((tf, d), lambda i, j: (j, 0)),
                  _full(ln)],
        out_specs=row,
        scratch_shapes=[pltpu.VMEM((tm, d), BF16), pltpu.VMEM((tm, d), F32)],
        compiler_params=_params(("arbitrary", "arbitrary")),
        name="sq_relu_mlp_ln",
    )(x2d, mod, w1, w2, ln)


def _qkv_kernel(d, scale, transposed, x_ref, kvmod_ref, mod_ref, wk_ref, wv_ref, wq_ref, k_o, v_o, *extra):
    x = x_ref[...]
    kvmod = kvmod_ref[0]
    hkv = (x * (1.0 + kvmod[:, d:2 * d]) + kvmod[:, 0:d]).astype(BF16)
    k = _dot(hkv, wk_ref[...])
    v = _dot(hkv, wv_ref[...])
    v_o[...] = v
    mod = mod_ref[0]
    h = (x * (1.0 + mod[:, d:2 * d]) + mod[:, 0:d]).astype(BF16)
    q = _dot(h, wq_ref[...]) * scale
    if transposed:
        kb_o, vt_o, qt_o = extra
        k_o[0] = k.T
        kb_o[...] = k.astype(BF16)
        vt_o[0] = v.T.astype(BF16)
        qt_o[0] = q.T.astype(BF16)
    else:
        (q_o,) = extra
        k_o[...] = k
        q_o[...] = q


def _qkv(x2d, kvmod, mod, tiles_per_batch, tm, wk, wv, wq, scale, transposed):
    n, d = x2d.shape
    row = pl.BlockSpec((tm, d), lambda i: (i, 0))
    row_sds = jax.ShapeDtypeStruct((n, d), F32)
    if transposed:
        nb = n // (tiles_per_batch * tm)
        col = pl.BlockSpec((1, d, tm), lambda i: (i // tiles_per_batch, 0, i % tiles_per_batch))
        col_shape = (nb, d, tiles_per_batch * tm)
        t_sds = jax.ShapeDtypeStruct(col_shape, BF16)
        out_shape = [jax.ShapeDtypeStruct(col_shape, F32), row_sds, jax.ShapeDtypeStruct((n, d), BF16), t_sds, t_sds]
        out_specs = [col, row, row, col, col]
    else:
        out_shape = [row_sds] * 3
        out_specs = [row] * 3
    return pl.pallas_call(
        functools.partial(_qkv_kernel, d, scale, transposed),
        out_shape=out_shape,
        grid=(n // tm,),
        in_specs=[row, _mod_spec(kvmod, tiles_per_batch), _mod_spec(mod, tiles_per_batch), _full(wk), _full(wv), _full(wq)],
        out_specs=out_specs,
        compiler_params=_params(("arbitrary",)),
        name="qkv_proj",
    )(x2d, kvmod, mod, wk, wv, wq)


def _diff_lambda(lq_ref, lk_ref, lam_init):
    lq = lq_ref[...]
    lk = lk_ref[...]
    dots = jnp.sum(lq * lk, -1, keepdims=True)
    return jnp.exp(dots[0:1, :]) - jnp.exp(dots[1:2, :]) + lam_init


DENOM_ROWS = 16


def _attn_kernel(tq, lam_init, qi_tab, ki_tab, qt_ref, k_ref, vt_ref, lq_ref, lk_ref, g_ref, o_ref,
                 qs_ref, m_ref, acc_ref):
    t = pl.program_id(2)
    qi = qi_tab[t]
    ki = ki_tab[t]

    @pl.when(ki == 0)
    def _():
        qt = qt_ref[0]
        row = lax.broadcasted_iota(jnp.int32, qt.shape, 0)
        qs_ref[:, 0:tq] = jnp.where(row < B_HEAD, qt, jnp.zeros_like(qt))
        qs_ref[:, tq:2 * tq] = jnp.where(row >= B_HEAD, qt, jnp.zeros_like(qt))
        m_ref[...] = jnp.full(m_ref.shape, NEG, F32)
        acc_ref[...] = jnp.zeros_like(acc_ref)

    def update(diagonal):
        s = _dot(k_ref[0], qs_ref[...])
        if diagonal:
            kpos = lax.broadcasted_iota(jnp.int32, s.shape, 0)
            qpos = lax.broadcasted_iota(jnp.int32, s.shape, 1) & (tq - 1)
            s = jnp.where(kpos <= qpos, s, NEG)
        m_prev = m_ref[...]
        m_new = jnp.maximum(m_prev, jnp.max(s, 0, keepdims=True))
        alpha = jnp.exp2(m_prev - m_new)
        p = jnp.exp2(s - m_new).astype(BF16)
        vt_ones = jnp.concatenate([vt_ref[0], jnp.ones((DENOM_ROWS, vt_ref.shape[2]), BF16)], axis=0)
        acc_ref[...] = alpha * acc_ref[...] + _dot(vt_ones, p)
        m_ref[...] = m_new

    @pl.when(ki < qi)
    def _():
        update(False)

    @pl.when(ki == qi)
    def _():
        update(True)
        o_all = acc_ref[0:LANES, :] * (1.0 / acc_ref[LANES:LANES + 1, :])
        lam = _diff_lambda(lq_ref, lk_ref, lam_init)
        o = o_all[:, 0:tq] - lam * o_all[:, tq:2 * tq]
        on = o * lax.rsqrt(jnp.mean(o * o, 0, keepdims=True) + SUBLN_EPS) * g_ref[...] * (1.0 - lam_init)
        o_ref[0] = on.T.astype(o_ref.dtype)


def _diff_attn_prompt(qt, k, vt, lam_q, lam_k, g_col, lam_init, tq):
    b, s, d = k.shape
    heads = d // LANES
    nq = s // tq
    pairs = [(qi, ki) for qi in range(nq) for ki in range(qi + 1)]
    qi_tab = jnp.asarray([p[0] for p in pairs], jnp.int32)
    ki_tab = jnp.asarray([p[1] for p in pairs], jnp.int32)
    qt_spec = pl.BlockSpec((1, LANES, tq), lambda bi, h, t, qt_, kt_: (bi, h, qt_[t]))
    vt_spec = pl.BlockSpec((1, LANES, tq), lambda bi, h, t, qt_, kt_: (bi, h, kt_[t]))
    k_spec = pl.BlockSpec((1, tq, LANES), lambda bi, h, t, qt_, kt_: (bi, kt_[t], h))
    o_spec = pl.BlockSpec((1, tq, LANES), lambda bi, h, t, qt_, kt_: (bi, qt_[t], h))
    small = lambda a: pl.BlockSpec(a.shape, lambda *_: (0,) * a.ndim)
    grid_spec = pltpu.PrefetchScalarGridSpec(
        num_scalar_prefetch=2,
        grid=(b, heads, len(pairs)),
        in_specs=[qt_spec, k_spec, vt_spec, small(lam_q), small(lam_k), small(g_col)],
        out_specs=o_spec,
        scratch_shapes=[pltpu.VMEM((LANES, 2 * tq), BF16), pltpu.VMEM((1, 2 * tq), F32),
                        pltpu.VMEM((LANES + DENOM_ROWS, 2 * tq), F32)],
    )
    return pl.pallas_call(
        functools.partial(_attn_kernel, tq, lam_init),
        out_shape=jax.ShapeDtypeStruct((b, s, d), BF16),
        grid_spec=grid_spec,
        compiler_params=_params(("arbitrary",) * 3),
        name="diff_attn_prompt",
    )(qi_tab, ki_tab, qt, k, vt, lam_q, lam_k, g_col)


def _decode_kernel(pp, lam_init, heads, pt_ref, qcol_ref, qrow_ref, kn_ref, vn_ref, lq_ref, lk_ref, g_ref, *rest):
    k_refs = rest[:pp]
    v_refs = rest[pp:2 * pp]
    o_ref, qb_ref, m_ref, l_ref, acc_ref = rest[2 * pp:]
    step = pl.program_id(1)
    rows = 2 * heads
    d = qb_ref.shape[0]

    @pl.when(step == 0)
    def _():
        qb_ref[...] = jnp.broadcast_to(qcol_ref[0], qb_ref.shape)
        m_ref[...] = jnp.full(m_ref.shape, NEG, F32)
        l_ref[...] = jnp.zeros_like(l_ref)
        acc_ref[...] = jnp.zeros_like(acc_ref)

    r = lax.broadcasted_iota(jnp.int32, (rows, d), 0)
    c = lax.broadcasted_iota(jnp.int32, (rows, d), 1)
    sel = ((c >> 7) == (r & (heads - 1))) & (((c >> 6) & 1) == (r // heads))
    sel_b = jnp.where(sel, 1.0, 0.0).astype(BF16)
    row_head = lax.broadcasted_iota(jnp.int32, (rows, LANES), 0) & (heads - 1)

    s = jnp.concatenate([_dot(sel_b, (k_refs[i][0] * qb_ref[...]).astype(BF16)) for i in range(pp)], axis=1)
    m_prev = m_ref[...]
    m_new = jnp.maximum(m_prev, jnp.max(s, -1, keepdims=True))
    alpha = jnp.exp(m_prev - m_new)
    p = jnp.exp(s - m_new)
    l_ref[...] = alpha * l_ref[...] + jnp.sum(p, -1, keepdims=True)
    pb = p.astype(BF16)
    pv = jnp.zeros((rows, LANES), F32)
    for i in range(pp):
        pb_i = pb[:, i * PAGE_SIZE:(i + 1) * PAGE_SIZE]
        for h in range(heads):
            v_h = v_refs[i][0, pl.ds(h, PAGE_SIZE, stride=heads), :].astype(BF16)
            pv = pv + _dot(jnp.where(row_head == h, pb_i, jnp.zeros_like(pb_i)), v_h)
    acc_ref[...] = alpha * acc_ref[...] + pv
    m_ref[...] = m_new

    @pl.when(step == pl.num_programs(1) - 1)
    def _():
        s_new = jnp.sum(jnp.where(sel, qrow_ref[0] * kn_ref[0], 0.0), -1, keepdims=True)
        m_prev = m_ref[...]
        m_fin = jnp.maximum(m_prev, s_new)
        alpha = jnp.exp(m_prev - m_fin)
        p_new = jnp.exp(s_new - m_fin)
        l_fin = alpha * l_ref[...] + p_new
        o_all = (alpha * acc_ref[...] + p_new * vn_ref[0]) / l_fin
        lam = _diff_lambda(lq_ref, lk_ref, lam_init)
        o = o_all[0:heads] - lam * o_all[heads:rows]
        on = o * lax.rsqrt(jnp.mean(o * o, -1, keepdims=True) + SUBLN_EPS) * g_ref[...] * (1.0 - lam_init)
        o_ref[0] = on


def _diff_attn_decode(q, k_new, v_new, cache_kt, cache_v, page_table, lam_q, lam_k, g_row, lam_init, pp):
    bd, d = q.shape
    heads = d // LANES
    rows = 2 * heads
    n_pages = page_table.shape[1]
    at_b = lambda shape: pl.BlockSpec((1,) + shape, lambda b, s, pt: (b, 0, 0))

    def page_spec(shape, i):
        return pl.BlockSpec((1,) + shape, lambda b, s, pt: (pt[b, s * pp + i], 0, 0))

    small = lambda a: pl.BlockSpec(a.shape, lambda b, s, pt: (0,) * a.ndim)
    grid_spec = pltpu.PrefetchScalarGridSpec(
        num_scalar_prefetch=1,
        grid=(bd, n_pages // pp),
        in_specs=[at_b((d, 1)), at_b((1, d)), at_b((1, d)), at_b((rows, LANES)), small(lam_q), small(lam_k), small(g_row)]
                 + [page_spec((d, PAGE_SIZE), i) for i in range(pp)]
                 + [page_spec((PAGE_SIZE * heads, LANES), i) for i in range(pp)],
        out_specs=at_b((heads, LANES)),
        scratch_shapes=[pltpu.VMEM((d, PAGE_SIZE), F32), pltpu.VMEM((rows, 1), F32), pltpu.VMEM((rows, 1), F32),
                        pltpu.VMEM((rows, LANES), F32)],
    )
    v_rows = jnp.tile(v_new, (1, 2, 1))
    return pl.pallas_call(
        functools.partial(_decode_kernel, pp, lam_init, heads),
        out_shape=jax.ShapeDtypeStruct((bd, heads, LANES), F32),
        grid_spec=grid_spec,
        compiler_params=_params(("arbitrary", "arbitrary")),
        name="diff_attn_decode",
    )(page_table, q.reshape(bd, d, 1), q.reshape(bd, 1, d), k_new.reshape(bd, 1, d), v_rows, lam_q, lam_k, g_row,
      *([cache_kt] * pp), *([cache_v] * pp))


def _pad_to(a, axis, size):
    pad = [(0, 0)] * a.ndim
    pad[axis] = (0, size - a.shape[axis])
    return jnp.pad(a, pad)


def _pack_state(s):
    b, h, n, _ = s.shape
    return s.reshape(b, h // 2, 2, n, n).transpose(0, 1, 3, 2, 4).reshape(b, h // 2, n, 2 * n)


def _unpack_state(s):
    b, p, n, _ = s.shape
    return s.reshape(b, p, n, 2, n).transpose(0, 1, 3, 2, 4).reshape(b, 2 * p, n, n)


def kernel(x_prompt, x_sample, cache_k, cache_v, state_wkv, state_shift, page_table, c_prompt, c_sample, ln_g, ln_b, ada_w, ada_b, ffn_w1, ffn_w2, a_mu, a_w_rkv, a_w_o, a_w0, a_w1, a_w2, a_a0, a_a1, a_a2, a_g1, a_g2, a_k_k, a_k_a, a_r_k, a_gn_g, a_gn_b, kv_ada_w, kv_ada_b, kv_w_k, kv_w_v, b_w_q, b_w_o, b_lam_q, b_lam_k, b_subln_g):
    bp, seq, d = x_prompt.shape
    bd, dec_seq, _ = x_sample.shape
    depth = ada_w.shape[0]
    n_a = a_mu.shape[0]
    assert depth == 2 and n_a == 1 and dec_seq == 1
    alpha = (2 * depth) ** 0.25
    heads = d // LANES
    pool = cache_k.shape[0]

    n_c = bp + bd
    c_all = _pad_to(jnp.concatenate([c_prompt, c_sample], axis=0), 0, -(-n_c // 8) * 8)
    mods = [_ada(c_all, ada_w, ada_b.reshape(depth, 1, -1), l) for l in range(depth)]
    kvmods = _ada(c_all, kv_ada_w[None], kv_ada_b.reshape(1, 1, -1), 0)

    wrkv = a_w_rkv[0].astype(BF16)
    lora = 128
    w1 = _pad_to(a_w1[0], 1, lora).astype(BF16)
    w2 = _pad_to(a_w2[0], 0, lora).astype(BF16)
    a1 = _pad_to(a_a1[0], 1, lora).astype(BF16)
    a2 = _pad_to(a_a2[0], 0, lora).astype(BF16)
    g1 = _pad_to(a_g1[0], 1, 2 * lora).astype(BF16)
    g2 = _pad_to(a_g2[0], 0, 2 * lora).astype(BF16)
    vec = _pad_to(jnp.stack([a_w0[0], a_a0[0], a_k_k[0], a_k_a[0], a_r_k[0].reshape(d)]), 0, 8)
    gn = jnp.stack([a_gn_g[0], a_gn_b[0]])
    a_wo = a_w_o[0].astype(BF16)
    ffn1 = ffn_w1.astype(BF16)
    ffn2 = ffn_w2.astype(BF16)
    wk = kv_w_k.astype(BF16)
    wv = kv_w_v.astype(BF16)
    wq = b_w_q[0].astype(BF16)
    b_wo = b_w_o[0].astype(BF16)
    g_head = b_subln_g[0].reshape(1, LANES)
    lam_init = 0.8 - 0.6 * math.exp(-0.3 * n_a)
    ln = [[jnp.stack([ln_g[l, s], ln_b[l, s]]) for s in range(2)] for l in range(depth)]
    attn_scale = B_HEAD ** -0.5

    def trunk(x, rows, shift0, wkv0, seq_len, tm, tm_mlp, tc, attend):
        b = x.shape[0]
        x2d = x.reshape(b * seq_len, d)
        tiles = max(seq_len // tm, 1)
        tiles_mlp = max(seq_len // tm_mlp, 1)
        if seq_len == 1:
            shape_mod = lambda m: m[rows].reshape(1, b, -1)
            sh0 = shift0.reshape(1, b, d)
        else:
            shape_mod = lambda m: m[rows].reshape(b, 1, -1)
            sh0 = shift0.reshape(b, 1, d)
        mod0, mod1, kvmod = shape_mod(mods[0]), shape_mod(mods[1]), shape_mod(kvmods)

        *seqs, g, bonus, last = _rwkv_proj(x2d, mod0, sh0, seq_len, tm, a_mu[0], wrkv, w1, w2, a1, a2, g1, g2, vec)
        seqs = [s.reshape(b, seq_len, d) for s in seqs]
        if seq_len == 1:
            y, s_fin = _wkv(seqs, _pack_state(wkv0), tc, 2)
            s_fin = _unpack_state(s_fin)
        else:
            y, s_fin = _wkv_chunked(seqs, _pair_block_diag(wkv0), 2)
            s_fin = _pair_diag_blocks(s_fin)
        x1 = _mix_out(True, alpha, [y.reshape(b * seq_len, d), g, bonus], x2d, mod0, tiles, tm, a_wo, gn, ln[0][0])
        x2 = _mlp(alpha, x1, mod0, tiles_mlp, tm_mlp, ffn1[0], ffn2[0], ln[0][1])
        if seq_len > 1:
            kt, v, kb, vt, qt = _qkv(x2, kvmod, mod1, tiles, tm, wk, wv, wq, attn_scale * LOG2E, True)
            k_out = kt.reshape(b, heads, 2, B_HEAD, seq_len).transpose(0, 4, 1, 2, 3)
            on = attend(kb, vt, qt)
        else:
            k, v, q = _qkv(x2, kvmod, mod1, tiles, tm, wk, wv, wq, attn_scale, False)
            k_out = k.reshape(b, seq_len, heads, 2, B_HEAD)
            on = attend(k, v, q)
        x3 = _mix_out(False, alpha, [on], x2, mod1, tiles, tm, b_wo, gn, ln[1][0])
        y_out = _mlp(alpha, x3, mod1, tiles_mlp, tm_mlp, ffn1[1], ffn2[1], ln[1][1])
        return (y_out.reshape(b, seq_len, d), k_out, v.reshape(b, seq_len, heads, 2 * B_HEAD),
                s_fin[None], last.reshape(1, b, d))

    def attend_prompt(kb, vt, qt):
        on = _diff_attn_prompt(qt, kb.reshape(bp, seq, d), vt, b_lam_q[0], b_lam_k[0], g_head.reshape(LANES, 1),
                               lam_init, min(1024, seq))
        return on.reshape(bp * seq, d)

    def attend_sample(k, v, q):
        cache_kt = jnp.transpose(cache_k, (0, 2, 3, 4, 1)).reshape(pool, d, PAGE_SIZE)
        on = _diff_attn_decode(q, k, v.reshape(bd, heads, LANES), cache_kt,
                               cache_v.reshape(pool, PAGE_SIZE * heads, LANES), page_table,
                               b_lam_q[0], b_lam_k[0], g_head, lam_init, min(8, page_table.shape[1]))
        return on.reshape(bd, d)

    zeros_shift = jnp.zeros((bp, d), F32)
    zeros_wkv = jnp.zeros((bp, d // A_HEAD, A_HEAD, A_HEAD), F32)
    y_p, k_p, v_p, wkv_p, shift_p = trunk(x_prompt, slice(0, bp), zeros_shift, zeros_wkv, seq,
                                          min(256, seq), min(512, seq), min(128, seq), attend_prompt)
    y_s, k_s, v_s, wkv_s, shift_s = trunk(x_sample, slice(bp, bp + bd), state_shift[0], state_wkv[0], 1,
                                          bd, bd, 1, attend_sample)
    return (y_p, y_s, k_p, v_p, wkv_p, shift_p, k_s, v_s, wkv_s, shift_s)
```

```python
import functools
import math

import jax
import jax.numpy as jnp
from jax import lax
from jax.experimental import pallas as pl
from jax.experimental.pallas import tpu as pltpu

F32 = jnp.float32
BF16 = jnp.bfloat16

LANES = 128
A_HEAD = 64
B_HEAD = 64
PAGE_SIZE = 128
GN_EPS = 64e-5
SUBLN_EPS = 1e-5
LN_EPS = 1e-5
NEG = -1e30
LOG2E = math.log2(math.e)
VMEM_LIMIT = 56 * 2**20


def _params(sem):
    return pltpu.CompilerParams(dimension_semantics=sem, vmem_limit_bytes=VMEM_LIMIT)


def _dot(a, b):
    return jnp.dot(a, b, preferred_element_type=F32)


def _split(x):
    hi = x.astype(BF16)
    lo = (x - hi.astype(F32)).astype(BF16)
    return hi, lo


def _sigmoid(z):
    return 1.0 / (1.0 + jnp.exp(-z))


def _layer_norm(z, g, b):
    mu = jnp.mean(z, -1, keepdims=True)
    d = z - mu
    var = jnp.mean(d * d, -1, keepdims=True)
    return d * lax.rsqrt(var + LN_EPS) * g + b


def _pair_ones():
    r = lax.broadcasted_iota(jnp.int32, (LANES, LANES), 0) >> 6
    c = lax.broadcasted_iota(jnp.int32, (LANES, LANES), 1) >> 6
    return jnp.where(r == c, 1.0, 0.0).astype(BF16)


def _head_sum(x, ones):
    m, d = x.shape
    cols = d // LANES
    stacked = jnp.concatenate([x[:, c * LANES:(c + 1) * LANES] for c in range(cols)], axis=0)
    hi, lo = _split(stacked)
    s = _dot(hi, ones) + _dot(lo, ones)
    return jnp.concatenate([s[c * m:(c + 1) * m] for c in range(cols)], axis=1)


def _mod_spec(mod, tiles_per_batch):
    return pl.BlockSpec((1, mod.shape[1], mod.shape[2]), lambda i, *_: (i // tiles_per_batch, 0, 0))


def _full(a):
    nd = a.ndim
    return pl.BlockSpec(a.shape, lambda *_: (0,) * nd)


def _ada_kernel(c_ref, w_ref, b_ref, o_ref):
    c = c_ref[...]
    s = c * _sigmoid(c)
    o_ref[...] = _dot(s.astype(BF16), w_ref[0].astype(BF16)) + b_ref[0]


def _ada(c, w, b, layer):
    m, d = c.shape
    n = w.shape[2]
    tn = 1024
    return pl.pallas_call(
        _ada_kernel,
        out_shape=jax.ShapeDtypeStruct((m, n), F32),
        grid=(n // tn,),
        in_specs=[pl.BlockSpec((m, d), lambda j: (0, 0)),
                  pl.BlockSpec((1, d, tn), lambda j: (layer, 0, j)),
                  pl.BlockSpec((1, 1, tn), lambda j: (layer, 0, j))],
        out_specs=pl.BlockSpec((m, tn), lambda j: (0, j)),
        compiler_params=_params(("arbitrary",)),
        name="ada_proj",
    )(c, w, b)


def _rwkv_proj_kernel(single_step, tiles_per_batch, d,
                      x_ref, xp_ref, mod_ref, sh0_ref, mu_ref, wrkv_ref, w1_ref, w2_ref, a1_ref, a2_ref,
                      g1_ref, g2_ref, vec_ref,
                      r_o, w_o, k_o, v_o, a_o, b_o, g_o, bonus_o, last_o):
    mod = mod_ref[0]
    sh = mod[:, 0:d]
    sc = mod[:, d:2 * d]
    x = x_ref[...]
    h = x * (1.0 + sc) + sh
    if single_step:
        h_prev = sh0_ref[0]
        last_o[0] = h
    else:
        i = pl.program_id(0)
        prev_row = xp_ref[7:8, :] * (1.0 + sc) + sh
        first = jnp.where(i % tiles_per_batch == 0, sh0_ref[0], prev_row)
        rolled = pltpu.roll(h, 1, 0)
        rid = lax.broadcasted_iota(jnp.int32, h.shape, 0)
        h_prev = jnp.where(rid == 0, first, rolled)
        last_o[0] = h[h.shape[0] - 1:, :]
    xx = h_prev - h

    def mix(p):
        return (h + xx * mu_ref[p:p + 1, :]).astype(BF16)

    r = _dot(mix(0), wrkv_ref[0])
    k = _dot(mix(1), wrkv_ref[1])
    v = _dot(mix(2), wrkv_ref[2])
    lw = _dot(jnp.tanh(_dot(mix(3), w1_ref[...])).astype(BF16), w2_ref[...])
    la = _dot(_dot(mix(4), a1_ref[...]).astype(BF16), a2_ref[...])
    g = _dot(_sigmoid(_dot(mix(5), g1_ref[...])).astype(BF16), g2_ref[...])

    w0 = vec_ref[0:1, :]
    a0 = vec_ref[1:2, :]
    k_k = vec_ref[2:3, :]
    k_a = vec_ref[3:4, :]
    r_k = vec_ref[4:5, :]

    z = -(w0 + lw)
    softplus = jnp.maximum(z, 0.0) + jnp.log(1.0 + jnp.exp(-jnp.abs(z)))
    w_log = -softplus - 0.5
    log_decay = -jnp.exp(w_log)
    a_gate = _sigmoid(a0 + la)
    ones = _pair_ones()
    kkf = k * k_k
    norm = jnp.sqrt(_head_sum(kkf * kkf, ones))
    kk = kkf / jnp.maximum(norm, 1e-12)
    k_mod = k * (1.0 + (a_gate - 1.0) * k_a)
    bonus = _head_sum(r * k_mod * r_k, ones) * v

    w_o[...] = jnp.exp(log_decay) if single_step else log_decay
    for ref, val in ((r_o, r), (k_o, k_mod), (v_o, v), (a_o, -kk), (b_o, kk * a_gate), (g_o, g), (bonus_o, bonus)):
        ref[...] = val.astype(ref.dtype)


def _rwkv_proj(x2d, mod, shift0, seq_len, tm, mu, wrkv, w1, w2, a1, a2, g1, g2, vec):
    n, d = x2d.shape
    single = seq_len == 1
    tiles_per_batch = 1 if single else seq_len // tm
    nt = n // tm
    nb = mod.shape[0]
    last_rows = tm if single else 1
    row = pl.BlockSpec((tm, d), lambda i: (i, 0))
    if single:
        xp_spec = pl.BlockSpec((8, d), lambda i: (0, 0))
        sh0_spec = pl.BlockSpec((1, tm, d), lambda i: (i, 0, 0))
    else:
        xp_spec = pl.BlockSpec((8, d), lambda i: (jnp.maximum(i * (tm // 8) - 1, 0), 0))
        sh0_spec = pl.BlockSpec((1, 1, d), lambda i: (i // tiles_per_batch, 0, 0))
    sds = lambda dtype: jax.ShapeDtypeStruct((n, d), dtype)
    seq_dtype = F32 if single else BF16
    outs = pl.pallas_call(
        functools.partial(_rwkv_proj_kernel, single, tiles_per_batch, d),
        out_shape=[sds(seq_dtype), sds(F32)] + [sds(seq_dtype)] * 6 + [jax.ShapeDtypeStruct((nb, last_rows, d), F32)],
        grid=(nt,),
        in_specs=[row, xp_spec, _mod_spec(mod, tiles_per_batch), sh0_spec, _full(mu), _full(wrkv), _full(w1),
                  _full(w2), _full(a1), _full(a2), _full(g1), _full(g2), _full(vec)],
        out_specs=[row] * 8 + [pl.BlockSpec((1, last_rows, d), lambda i: (i // tiles_per_batch, 0, 0))],
        compiler_params=_params(("arbitrary",)),
        name="rwkv_proj",
    )(x2d, x2d, mod, shift0, mu, wrkv, w1, w2, a1, a2, g1, g2, vec)
    return outs


def _wkv_kernel(tc, r_ref, w_ref, k_ref, v_ref, a_ref, b_ref, s0_ref, y_ref, s_ref):
    @pl.when(pl.program_id(1) == 0)
    def _():
        s_ref[...] = s0_ref[...]

    groups, pairs = s_ref.shape[0], s_ref.shape[1]
    ones = _pair_ones()
    rid = lax.broadcasted_iota(jnp.int32, (A_HEAD, LANES), 0)
    cid = lax.broadcasted_iota(jnp.int32, (A_HEAD, LANES), 1) & (A_HEAD - 1)
    diag = rid == cid
    idx = [(g, p) for g in range(groups) for p in range(pairs)]
    n = len(idx)
    sets = [list(range(n))]

    rb = min(8, tc)
    sub = lax.broadcasted_iota(jnp.int32, (rb, LANES), 0)

    def block(tb, carry):
        base = pl.multiple_of(tb * rb, rb)

        def tile(ref, g, p):
            return ref[g, pl.ds(base, rb), p * LANES:(p + 1) * LANES]

        r_t, w_t, k_t, v_t, a_t, b_t = ([tile(ref, g, p) for g, p in idx]
                                        for ref in (r_ref, w_ref, k_ref, v_ref, a_ref, b_ref))
        states = [s_ref[g, p] for g, p in idx]
        y_tiles = [jnp.zeros((rb, LANES), F32) for _ in idx]
        for i in range(rb):
            row = lambda tiles, j: tiles[j][i:i + 1, :]
            for members in sets:
                sa_in = jnp.concatenate([(states[j] * row(a_t, j)).astype(BF16) for j in members], axis=0)
                vd = jnp.concatenate([jnp.where(diag, row(v_t, j), 0.0).astype(BF16) for j in members], axis=0)
                sa = _dot(sa_in, ones)
                vb = _dot(vd, ones)
                for q, j in enumerate(members):
                    sl = slice(q * A_HEAD, (q + 1) * A_HEAD)
                    states[j] = states[j] * row(w_t, j) + sa[sl] * row(b_t, j) + vb[sl] * row(k_t, j)
                y_in = jnp.concatenate([(states[j] * row(r_t, j)).astype(BF16) for j in members], axis=0)
                yb = _dot(y_in, ones)
                for q, j in enumerate(members):
                    sl = slice(q * A_HEAD, (q + 1) * A_HEAD)
                    y_row = jnp.sum(jnp.where(diag, yb[sl], 0.0), axis=0, keepdims=True)
                    y_tiles[j] = jnp.where(sub == i, y_row, y_tiles[j])
        for j, (g, p) in enumerate(idx):
            s_ref[g, p] = states[j]
            y_ref[g, pl.ds(base, rb), p * LANES:(p + 1) * LANES] = y_tiles[j]
        return carry

    lax.fori_loop(0, tc // rb, block, 0)


def _wkv(seqs, s0, tc, group):
    b, t, d = seqs[0].shape
    pairs = s0.shape[1]
    seq_spec = pl.BlockSpec((group, tc, d), lambda i, j: (i, j, 0))
    st_spec = pl.BlockSpec((group, pairs, A_HEAD, LANES), lambda i, j: (i, 0, 0, 0))
    return pl.pallas_call(
        functools.partial(_wkv_kernel, tc),
        out_shape=[jax.ShapeDtypeStruct((b, t, d), F32), jax.ShapeDtypeStruct(s0.shape, F32)],
        grid=(b // group, t // tc),
        in_specs=[seq_spec] * 6 + [st_spec],
        out_specs=[seq_spec, st_spec],
        compiler_params=_params(("arbitrary", "arbitrary")),
        name="wkv7_scan",
    )(*seqs, s0)


WKV_CHUNK = 64


def _dot_nt(a, b):
    return lax.dot_general(a, b, (((1,), (1,)), ((), ())), preferred_element_type=F32)


def _dot_tn(a, b):
    return lax.dot_general(a, b, (((0,), (0,)), ((), ())), preferred_element_type=F32)


def _wkv_chunk_kernel(r_ref, lw_ref, k_ref, v_ref, a_ref, b_ref, s0_ref, y_ref, s_ref):
    @pl.when(pl.program_id(1) == 0)
    def _():
        s_ref[...] = s0_ref[...]

    groups, pairs = s_ref.shape[0], s_ref.shape[1]
    c = r_ref.shape[1]
    iota = lambda shape, dim: lax.broadcasted_iota(jnp.int32, shape, dim)
    head0 = iota((c, LANES), 1) < A_HEAD
    tri = jnp.where(iota((c, c), 1) <= iota((c, c), 0), 1.0, 0.0).astype(BF16)
    rows4 = iota((4 * c, LANES), 0)
    t_idx = rows4 & (c - 1)
    s_idx = iota((4 * c, LANES), 1) & (c - 1)
    keep = s_idx < t_idx + jnp.where(rows4 < 2 * c, 0, 1)
    same_head = (iota((LANES, LANES), 0) >> 6) == (iota((LANES, LANES), 1) >> 6)
    same_block = (iota((LANES, LANES), 0) >> 3) == (iota((LANES, LANES), 1) >> 3)
    eye = jnp.where(iota((LANES, LANES), 0) == iota((LANES, LANES), 1), 1.0, 0.0)

    def by_head(x):
        return jnp.concatenate([jnp.where(head0, x, 0.0), jnp.where(head0, 0.0, x)], axis=0)

    units = [(g, p) for g in range(groups) for p in range(pairs)]
    each = lambda f, *lists: [f(*args) for args in zip(*lists)]
    bf = lambda xs: [x.astype(BF16) for x in xs]
    swap = lambda x: pltpu.roll(x, A_HEAD, 1)

    def load(ref):
        return [ref[g, :, p * LANES:(p + 1) * LANES].astype(F32) for g, p in units]

    r, lw, k, v, a, b = (load(ref) for ref in (r_ref, lw_ref, k_ref, v_ref, a_ref, b_ref))
    s_old = [s_ref[g, p] for g, p in units]

    def cumulative(x):
        hi, lo = _split(x)
        return _dot(tri, hi) + _dot(tri, lo)

    cum = each(cumulative, lw)
    end = [x[c - 1:c, :] for x in cum]
    inv = [jnp.exp(-x) for x in cum]
    rest = each(lambda e, x: jnp.exp(e - x), end, cum)
    at = each(lambda a_, x, l_: a_ * jnp.exp(x - l_), a, cum, lw)
    rt = each(lambda r_, x: r_ * jnp.exp(x), r, cum)
    bt = each(jnp.multiply, b, inv)
    kt = each(jnp.multiply, k, inv)
    bb = bf(each(jnp.multiply, b, rest))
    kb = bf(each(jnp.multiply, k, rest))
    at_b, rt_b, v_b = bf(at), bf(rt), bf(v)
    lhs = each(lambda x, y: jnp.concatenate([by_head(x), by_head(y)], axis=0).astype(BF16), at, rt)
    rhs = each(lambda x, y: jnp.concatenate([x, y], axis=0).astype(BF16), bt, kt)
    aa = each(lambda x, y: jnp.where(keep, _dot_nt(x, y), 0.0), lhs, rhs)
    a0, a1, r0, r1 = ([x[i * c:(i + 1) * c] for x in aa] for i in range(4))
    n = each(lambda x, y: jnp.concatenate([jnp.where(head0, x, 0.0), jnp.where(head0, 0.0, swap(y))], axis=0), a0, a1)
    a_ak = bf(each(lambda x, y: jnp.where(head0, swap(x), y), a0, a1))
    a_rb = bf(each(lambda x, y: jnp.where(head0, x, swap(y)), r0, r1))
    a_rk = bf(each(lambda x, y: jnp.where(head0, swap(x), y), r0, r1))
    mm = lambda p_, q_: _dot(p_.astype(BF16), q_.astype(BF16))

    def neumann8(z):
        out = [eye + z_ for z_ in z]
        for _ in range(2):
            z = each(mm, z, z)
            out = each(lambda o_, z_: o_ + mm(o_, z_), out, z)
        return out

    d_in = [jnp.where(same_block, n_, 0.0) for n_ in n]
    d_inv = neumann8(d_in)
    m_mat = each(lambda di_, n_, d_: mm(di_, n_ - d_), d_inv, n, d_in)
    inv_t = each(mm, neumann8(m_mat), d_inv)
    v_heads = bf(each(by_head, v))
    s_b = bf(s_old)
    rhs0 = each(lambda at_, s_, ak_, vh_: _dot_nt(at_, s_) + _dot(ak_, vh_), at_b, s_b, a_ak, v_heads)
    uu = each(lambda t_, x_: _dot(t_.astype(BF16), by_head(x_).astype(BF16)), inv_t, rhs0)
    u = [x_[0:c] + x_[c:2 * c] for x_ in uu]
    u_heads = bf(each(by_head, u))
    y = each(lambda rt_, s_, rb_, uh_, rk_, vh_: _dot_nt(rt_, s_) + _dot(rb_, uh_) + _dot(rk_, vh_),
             rt_b, s_b, a_rb, u_heads, a_rk, v_heads)
    upd = each(lambda u_, bb_, v_, kb_: _dot_tn(u_, bb_) + _dot_tn(v_, kb_), bf(u), bb, v_b, kb)
    s_new = each(lambda s_, e, up: s_ * jnp.exp(e) + jnp.where(same_head, up, 0.0), s_old, end, upd)
    for (g, p), y_, s_ in zip(units, y, s_new):
        y_ref[g, :, p * LANES:(p + 1) * LANES] = y_
        s_ref[g, p] = s_


def _wkv_chunked(seqs, s0, group):
    b, t, d = seqs[0].shape
    pairs = s0.shape[1]
    seq_spec = pl.BlockSpec((group, WKV_CHUNK, d), lambda i, j: (i, j, 0))
    st_spec = pl.BlockSpec((group, pairs, LANES, LANES), lambda i, j: (i, 0, 0, 0))
    return pl.pallas_call(
        _wkv_chunk_kernel,
        out_shape=[jax.ShapeDtypeStruct((b, t, d), F32), jax.ShapeDtypeStruct(s0.shape, F32)],
        grid=(b // group, t // WKV_CHUNK),
        in_specs=[seq_spec] * 6 + [st_spec],
        out_specs=[seq_spec, st_spec],
        compiler_params=_params(("arbitrary", "arbitrary")),
        name="wkv7_chunked",
    )(*seqs, s0)


def _pair_block_diag(s):
    b, h, n, _ = s.shape
    s = s.reshape(b, h // 2, 2, n, n)
    z = jnp.zeros_like(s[:, :, 0])
    top = jnp.concatenate([s[:, :, 0], z], axis=-1)
    bot = jnp.concatenate([z, s[:, :, 1]], axis=-1)
    return jnp.concatenate([top, bot], axis=-2)


def _pair_diag_blocks(s):
    b, p, _, _ = s.shape
    n = A_HEAD
    return jnp.stack([s[:, :, 0:n, 0:n], s[:, :, n:, n:]], axis=2).reshape(b, 2 * p, n, n)


def _mix_out_kernel(rwkv, alpha, d, *refs):
    if rwkv:
        y_ref, g_ref, bonus_ref, x_ref, mod_ref, wo_ref, gn_ref, ln_ref, o_ref = refs
        y = y_ref[...]
        ones = _pair_ones()
        mu_y = _head_sum(y, ones) * (1.0 / A_HEAD)
        dy = y - mu_y
        var_y = _head_sum(dy * dy, ones) * (1.0 / A_HEAD)
        yn = dy * lax.rsqrt(var_y + GN_EPS) * gn_ref[0:1, :] + gn_ref[1:2, :]
        pre = ((yn + bonus_ref[...].astype(F32)) * g_ref[...].astype(F32)).astype(BF16)
    else:
        on_ref, x_ref, mod_ref, wo_ref, ln_ref, o_ref = refs
        pre = on_ref[...].astype(BF16)
    mix = _dot(pre, wo_ref[...])
    gt = mod_ref[0][:, 2 * d:3 * d]
    z = alpha * x_ref[...] + (1.0 + gt) * mix
    o_ref[...] = _layer_norm(z, ln_ref[0:1, :], ln_ref[1:2, :])


def _mix_out(rwkv, alpha, ins, x2d, mod, tiles_per_batch, tm, wo, gn, ln):
    n, d = x2d.shape
    row = pl.BlockSpec((tm, d), lambda i: (i, 0))
    extra = [gn] if rwkv else []
    args = list(ins) + [x2d, mod, wo] + extra + [ln]
    specs = [row] * (len(ins) + 1) + [_mod_spec(mod, tiles_per_batch), _full(wo)] + [_full(e) for e in extra] + [_full(ln)]
    return pl.pallas_call(
        functools.partial(_mix_out_kernel, rwkv, alpha, d),
        out_shape=jax.ShapeDtypeStruct((n, d), F32),
        grid=(n // tm,),
        in_specs=specs,
        out_specs=row,
        compiler_params=_params(("arbitrary",)),
        name="rwkv_out_ln" if rwkv else "attn_out_ln",
    )(*args)


def _mlp_kernel(alpha, d, x_ref, mod_ref, w1_ref, w2_ref, ln_ref, o_ref, hf_ref, acc_ref):
    j = pl.program_id(1)
    mod = mod_ref[0]

    @pl.when(j == 0)
    def _():
        hf_ref[...] = (x_ref[...] * (1.0 + mod[:, 4 * d:5 * d]) + mod[:, 3 * d:4 * d]).astype(BF16)
        acc_ref[...] = jnp.zeros_like(acc_ref)

    h1 = jnp.maximum(_dot(hf_ref[...], w1_ref[...]), 0.0)
    acc_ref[...] += _dot((h1 * h1).astype(BF16), w2_ref[...])

    @pl.when(j == pl.num_programs(1) - 1)
    def _():
        z = alpha * x_ref[...] + (1.0 + mod[:, 5 * d:6 * d]) * acc_ref[...]
        o_ref[...] = _layer_norm(z, ln_ref[0:1, :], ln_ref[1:2, :])


def _mlp(alpha, x2d, mod, tiles_per_batch, tm, w1, w2, ln):
    n, d = x2d.shape
    dff = w1.shape[1]
    tf = 1024
    row = pl.BlockSpec((tm, d), lambda i, j: (i, 0))
    return pl.pallas_call(
        functools.partial(_mlp_kernel, alpha, d),
        out_shape=jax.ShapeDtypeStruct((n, d), F32),
        grid=(n // tm, dff // tf),
        in_specs=[row, _mod_spec(mod, tiles_per_batch),
                  pl.BlockSpec((d, tf), lambda i, j: (0, j)),
                  pl.BlockSpec((tf, d), lambda i, j: (j, 0)),
                  _full(ln)],
        out_specs=row,
        scratch_shapes=[pltpu.VMEM((tm, d), BF16), pltpu.VMEM((tm, d), F32)],
        compiler_params=_params(("arbitrary", "arbitrary")),
        name="sq_relu_mlp_ln",
    )(x2d, mod, w1, w2, ln)


def _qkv_kernel(d, scale, transposed, x_ref, kvmod_ref, mod_ref, wk_ref, wv_ref, wq_ref, k_o, v_o, *extra):
    x = x_ref[...]
    kvmod = kvmod_ref[0]
    hkv = (x * (1.0 + kvmod[:, d:2 * d]) + kvmod[:, 0:d]).astype(BF16)
    k = _dot(hkv, wk_ref[...])
    v = _dot(hkv, wv_ref[...])
    v_o[...] = v
    mod = mod_ref[0]
    h = (x * (1.0 + mod[:, d:2 * d]) + mod[:, 0:d]).astype(BF16)
    q = _dot(h, wq_ref[...]) * scale
    if transposed:
        kb_o, vt_o, qt_o = extra
        k_o[0] = k.T
        kb_o[...] = k.astype(BF16)
        vt_o[0] = v.T.astype(BF16)
        qt_o[0] = q.T.astype(BF16)
    else:
        (q_o,) = extra
        k_o[...] = k
        q_o[...] = q


def _qkv(x2d, kvmod, mod, tiles_per_batch, tm, wk, wv, wq, scale, transposed):
    n, d = x2d.shape
    row = pl.BlockSpec((tm, d), lambda i: (i, 0))
    row_sds = jax.ShapeDtypeStruct((n, d), F32)
    if transposed:
        nb = n // (tiles_per_batch * tm)
        col = pl.BlockSpec((1, d, tm), lambda i: (i // tiles_per_batch, 0, i % tiles_per_batch))
        col_shape = (nb, d, tiles_per_batch * tm)
        t_sds = jax.ShapeDtypeStruct(col_shape, BF16)
        out_shape = [jax.ShapeDtypeStruct(col_shape, F32), row_sds, jax.ShapeDtypeStruct((n, d), BF16), t_sds, t_sds]
        out_specs = [col, row, row, col, col]
    else:
        out_shape = [row_sds] * 3
        out_specs = [row] * 3
    return pl.pallas_call(
        functools.partial(_qkv_kernel, d, scale, transposed),
        out_shape=out_shape,
        grid=(n // tm,),
        in_specs=[row, _mod_spec(kvmod, tiles_per_batch), _mod_spec(mod, tiles_per_batch), _full(wk), _full(wv), _full(wq)],
        out_specs=out_specs,
        compiler_params=_params(("arbitrary",)),
        name="qkv_proj",
    )(x2d, kvmod, mod, wk, wv, wq)


def _diff_lambda(lq_ref, lk_ref, lam_init):
    lq = lq_ref[...]
    lk = lk_ref[...]
    dots = jnp.sum(lq * lk, -1, keepdims=True)
    return jnp.exp(dots[0:1, :]) - jnp.exp(dots[1:2, :]) + lam_init


DENOM_ROWS = 16


def _attn_kernel(tq, lam_init, qi_tab, ki_tab, qt_ref, k_ref, vt_ref, lq_ref, lk_ref, g_ref, o_ref,
                 qs_ref, m_ref, acc_ref):
    t = pl.program_id(2)
    qi = qi_tab[t]
    ki = ki_tab[t]

    @pl.when(ki == 0)
    def _():
        qt = qt_ref[0]
        row = lax.broadcasted_iota(jnp.int32, qt.shape, 0)
        qs_ref[:, 0:tq] = jnp.where(row < B_HEAD, qt, jnp.zeros_like(qt))
        qs_ref[:, tq:2 * tq] = jnp.where(row >= B_HEAD, qt, jnp.zeros_like(qt))
        m_ref[...] = jnp.full(m_ref.shape, NEG, F32)
        acc_ref[...] = jnp.zeros_like(acc_ref)

    def update(diagonal):
        s = _dot(k_ref[0], qs_ref[...])
        if diagonal:
            kpos = lax.broadcasted_iota(jnp.int32, s.shape, 0)
            qpos = lax.broadcasted_iota(jnp.int32, s.shape, 1) & (tq - 1)
            s = jnp.where(kpos <= qpos, s, NEG)
        m_prev = m_ref[...]
        m_new = jnp.maximum(m_prev, jnp.max(s, 0, keepdims=True))
        alpha = jnp.exp2(m_prev - m_new)
        p = jnp.exp2(s - m_new).astype(BF16)
        vt_ones = jnp.concatenate([vt_ref[0], jnp.ones((DENOM_ROWS, vt_ref.shape[2]), BF16)], axis=0)
        acc_ref[...] = alpha * acc_ref[...] + _dot(vt_ones, p)
        m_ref[...] = m_new

    @pl.when(ki < qi)
    def _():
        update(False)

    @pl.when(ki == qi)
    def _():
        update(True)
        o_all = acc_ref[0:LANES, :] * (1.0 / acc_ref[LANES:LANES + 1, :])
        lam = _diff_lambda(lq_ref, lk_ref, lam_init)
        o = o_all[:, 0:tq] - lam * o_all[:, tq:2 * tq]
        on = o * lax.rsqrt(jnp.mean(o * o, 0, keepdims=True) + SUBLN_EPS) * g_ref[...] * (1.0 - lam_init)
        o_ref[0] = on.T.astype(o_ref.dtype)


def _diff_attn_prompt(qt, k, vt, lam_q, lam_k, g_col, lam_init, tq):
    b, s, d = k.shape
    heads = d // LANES
    nq = s // tq
    pairs = [(qi, ki) for qi in range(nq) for ki in range(qi + 1)]
    qi_tab = jnp.asarray([p[0] for p in pairs], jnp.int32)
    ki_tab = jnp.asarray([p[1] for p in pairs], jnp.int32)
    qt_spec = pl.BlockSpec((1, LANES, tq), lambda bi, h, t, qt_, kt_: (bi, h, qt_[t]))
    vt_spec = pl.BlockSpec((1, LANES, tq), lambda bi, h, t, qt_, kt_: (bi, h, kt_[t]))
    k_spec = pl.BlockSpec((1, tq, LANES), lambda bi, h, t, qt_, kt_: (bi, kt_[t], h))
    o_spec = pl.BlockSpec((1, tq, LANES), lambda bi, h, t, qt_, kt_: (bi, qt_[t], h))
    small = lambda a: pl.BlockSpec(a.shape, lambda *_: (0,) * a.ndim)
    grid_spec = pltpu.PrefetchScalarGridSpec(
        num_scalar_prefetch=2,
        grid=(b, heads, len(pairs)),
        in_specs=[qt_spec, k_spec, vt_spec, small(lam_q), small(lam_k), small(g_col)],
        out_specs=o_spec,
        scratch_shapes=[pltpu.VMEM((LANES, 2 * tq), BF16), pltpu.VMEM((1, 2 * tq), F32),
                        pltpu.VMEM((LANES + DENOM_ROWS, 2 * tq), F32)],
    )
    return pl.pallas_call(
        functools.partial(_attn_kernel, tq, lam_init),
        out_shape=jax.ShapeDtypeStruct((b, s, d), BF16),
        grid_spec=grid_spec,
        compiler_params=_params(("arbitrary",) * 3),
        name="diff_attn_prompt",
    )(qi_tab, ki_tab, qt, k, vt, lam_q, lam_k, g_col)


def _decode_kernel(pp, lam_init, heads, pt_ref, qcol_ref, qrow_ref, kn_ref, vn_ref, lq_ref, lk_ref, g_ref, *rest):
    k_refs = rest[:pp]
    v_refs = rest[pp:2 * pp]
    o_ref, qb_ref, m_ref, l_ref, acc_ref = rest[2 * pp:]
    step = pl.program_id(1)
    rows = 2 * heads
    d = qb_ref.shape[0]

    @pl.when(step == 0)
    def _():
        qb_ref[...] = jnp.broadcast_to(qcol_ref[0], qb_ref.shape)
        m_ref[...] = jnp.full(m_ref.shape, NEG, F32)
        l_ref[...] = jnp.zeros_like(l_ref)
        acc_ref[...] = jnp.zeros_like(acc_ref)

    r = lax.broadcasted_iota(jnp.int32, (rows, d), 0)
    c = lax.broadcasted_iota(jnp.int32, (rows, d), 1)
    sel = ((c >> 7) == (r & (heads - 1))) & (((c >> 6) & 1) == (r // heads))
    sel_b = jnp.where(sel, 1.0, 0.0).astype(BF16)
    row_head = lax.broadcasted_iota(jnp.int32, (rows, LANES), 0) & (heads - 1)

    s = jnp.concatenate([_dot(sel_b, (k_refs[i][0] * qb_ref[...]).astype(BF16)) for i in range(pp)], axis=1)
    m_prev = m_ref[...]
    m_new = jnp.maximum(m_prev, jnp.max(s, -1, keepdims=True))
    alpha = jnp.exp(m_prev - m_new)
    p = jnp.exp(s - m_new)
    l_ref[...] = alpha * l_ref[...] + jnp.sum(p, -1, keepdims=True)
    pb = p.astype(BF16)
    pv = jnp.zeros((rows, LANES), F32)
    for i in range(pp):
        pb_i = pb[:, i * PAGE_SIZE:(i + 1) * PAGE_SIZE]
        for h in range(heads):
            v_h = v_refs[i][0, pl.ds(h, PAGE_SIZE, stride=heads), :].astype(BF16)
            pv = pv + _dot(jnp.where(row_head == h, pb_i, jnp.zeros_like(pb_i)), v_h)
    acc_ref[...] = alpha * acc_ref[...] + pv
    m_ref[...] = m_new

    @pl.when(step == pl.num_programs(1) - 1)
    def _():
        s_new = jnp.sum(jnp.where(sel, qrow_ref[0] * kn_ref[0], 0.0), -1, keepdims=True)
        m_prev = m_ref[...]
        m_fin = jnp.maximum(m_prev, s_new)
        alpha = jnp.exp(m_prev - m_fin)
        p_new = jnp.exp(s_new - m_fin)
        l_fin = alpha * l_ref[...] + p_new
        o_all = (alpha * acc_ref[...] + p_new * vn_ref[0]) / l_fin
        lam = _diff_lambda(lq_ref, lk_ref, lam_init)
        o = o_all[0:heads] - lam * o_all[heads:rows]
        on = o * lax.rsqrt(jnp.mean(o * o, -1, keepdims=True) + SUBLN_EPS) * g_ref[...] * (1.0 - lam_init)
        o_ref[0] = on


def _diff_attn_decode(q, k_new, v_new, cache_kt, cache_v, page_table, lam_q, lam_k, g_row, lam_init, pp):
    bd, d = q.shape
    heads = d // LANES
    rows = 2 * heads
    n_pages = page_table.shape[1]
    at_b = lambda shape: pl.BlockSpec((1,) + shape, lambda b, s, pt: (b, 0, 0))

    def page_spec(shape, i):
        return pl.BlockSpec((1,) + shape, lambda b, s, pt: (pt[b, s * pp + i], 0, 0))

    small = lambda a: pl.BlockSpec(a.shape, lambda b, s, pt: (0,) * a.ndim)
    grid_spec = pltpu.PrefetchScalarGridSpec(
        num_scalar_prefetch=1,
        grid=(bd, n_pages // pp),
        in_specs=[at_b((d, 1)), at_b((1, d)), at_b((1, d)), at_b((rows, LANES)), small(lam_q), small(lam_k), small(g_row)]
                 + [page_spec((d, PAGE_SIZE), i) for i in range(pp)]
                 + [page_spec((PAGE_SIZE * heads, LANES), i) for i in range(pp)],
        out_specs=at_b((heads, LANES)),
        scratch_shapes=[pltpu.VMEM((d, PAGE_SIZE), F32), pltpu.VMEM((rows, 1), F32), pltpu.VMEM((rows, 1), F32),
                        pltpu.VMEM((rows, LANES), F32)],
    )
    v_rows = jnp.tile(v_new, (1, 2, 1))
    return pl.pallas_call(
        functools.partial(_decode_kernel, pp, lam_init, heads),
        out_shape=jax.ShapeDtypeStruct((bd, heads, LANES), F32),
        grid_spec=grid_spec,
        compiler_params=_params(("arbitrary", "arbitrary")),
        name="diff_attn_decode",
    )(page_table, q.reshape(bd, d, 1), q.reshape(bd, 1, d), k_new.reshape(bd, 1, d), v_rows, lam_q, lam_k, g_row,
      *([cache_kt] * pp), *([cache_v] * pp))


def _pad_to(a, axis, size):
    pad = [(0, 0)] * a.ndim
    pad[axis] = (0, size - a.shape[axis])
    return jnp.pad(a, pad)


def _pack_state(s):
    b, h, n, _ = s.shape
    return s.reshape(b, h // 2, 2, n, n).transpose(0, 1, 3, 2, 4).reshape(b, h // 2, n, 2 * n)


def _unpack_state(s):
    b, p, n, _ = s.shape
    return s.reshape(b, p, n, 2, n).transpose(0, 1, 3, 2, 4).reshape(b, 2 * p, n, n)


def kernel(x_prompt, x_sample, cache_k, cache_v, state_wkv, state_shift, page_table, c_prompt, c_sample, ln_g, ln_b, ada_w, ada_b, ffn_w1, ffn_w2, a_mu, a_w_rkv, a_w_o, a_w0, a_w1, a_w2, a_a0, a_a1, a_a2, a_g1, a_g2, a_k_k, a_k_a, a_r_k, a_gn_g, a_gn_b, kv_ada_w, kv_ada_b, kv_w_k, kv_w_v, b_w_q, b_w_o, b_lam_q, b_lam_k, b_subln_g):
    bp, seq, d = x_prompt.shape
    bd, dec_seq, _ = x_sample.shape
    depth = ada_w.shape[0]
    n_a = a_mu.shape[0]
    assert depth == 2 and n_a == 1 and dec_seq == 1
    alpha = (2 * depth) ** 0.25
    heads = d // LANES
    pool = cache_k.shape[0]

    n_c = bp + bd
    c_all = _pad_to(jnp.concatenate([c_prompt, c_sample], axis=0), 0, -(-n_c // 8) * 8)
    mods = [_ada(c_all, ada_w, ada_b.reshape(depth, 1, -1), l) for l in range(depth)]
    kvmods = _ada(c_all, kv_ada_w[None], kv_ada_b.reshape(1, 1, -1), 0)

    wrkv = a_w_rkv[0].astype(BF16)
    lora = 128
    w1 = _pad_to(a_w1[0], 1, lora).astype(BF16)
    w2 = _pad_to(a_w2[0], 0, lora).astype(BF16)
    a1 = _pad_to(a_a1[0], 1, lora).astype(BF16)
    a2 = _pad_to(a_a2[0], 0, lora).astype(BF16)
    g1 = _pad_to(a_g1[0], 1, 2 * lora).astype(BF16)
    g2 = _pad_to(a_g2[0], 0, 2 * lora).astype(BF16)
    vec = _pad_to(jnp.stack([a_w0[0], a_a0[0], a_k_k[0], a_k_a[0], a_r_k[0].reshape(d)]), 0, 8)
    gn = jnp.stack([a_gn_g[0], a_gn_b[0]])
    a_wo = a_w_o[0].astype(BF16)
    ffn1 = ffn_w1.astype(BF16)
    ffn2 = ffn_w2.astype(BF16)
    wk = kv_w_k.astype(BF16)
    wv = kv_w_v.astype(BF16)
    wq = b_w_q[0].astype(BF16)
    b_wo = b_w_o[0].astype(BF16)
    g_head = b_subln_g[0].reshape(1, LANES)
    lam_init = 0.8 - 0.6 * math.exp(-0.3 * n_a)
    ln = [[jnp.stack([ln_g[l, s], ln_b[l, s]]) for s in range(2)] for l in range(depth)]
    attn_scale = B_HEAD ** -0.5

    def trunk(x, rows, shift0, wkv0, seq_len, tm, tm_mlp, tc, attend):
        b = x.shape[0]
        x2d = x.reshape(b * seq_len, d)
        tiles = max(seq_len // tm, 1)
        tiles_mlp = max(seq_len // tm_mlp, 1)
        if seq_len == 1:
            shape_mod = lambda m: m[rows].reshape(1, b, -1)
            sh0 = shift0.reshape(1, b, d)
        else:
            shape_mod = lambda m: m[rows].reshape(b, 1, -1)
            sh0 = shift0.reshape(b, 1, d)
        mod0, mod1, kvmod = shape_mod(mods[0]), shape_mod(mods[1]), shape_mod(kvmods)

        *seqs, g, bonus, last = _rwkv_proj(x2d, mod0, sh0, seq_len, tm, a_mu[0], wrkv, w1, w2, a1, a2, g1, g2, vec)
        seqs = [s.reshape(b, seq_len, d) for s in seqs]
        if seq_len == 1:
            y, s_fin = _wkv(seqs, _pack_state(wkv0), tc, 2)
            s_fin = _unpack_state(s_fin)
        else:
            y, s_fin = _wkv_chunked(seqs, _pair_block_diag(wkv0), 2)
            s_fin = _pair_diag_blocks(s_fin)
        x1 = _mix_out(True, alpha, [y.reshape(b * seq_len, d), g, bonus], x2d, mod0, tiles, tm, a_wo, gn, ln[0][0])
        x2 = _mlp(alpha, x1, mod0, tiles_mlp, tm_mlp, ffn1[0], ffn2[0], ln[0][1])
        if seq_len > 1:
            kt, v, kb, vt, qt = _qkv(x2, kvmod, mod1, tiles, tm, wk, wv, wq, attn_scale * LOG2E, True)
            k_out = kt.reshape(b, heads, 2, B_HEAD, seq_len).transpose(0, 4, 1, 2, 3)
            on = attend(kb, vt, qt)
        else:
            k, v, q = _qkv(x2, kvmod, mod1, tiles, tm, wk, wv, wq, attn_scale, False)
            k_out = k.reshape(b, seq_len, heads, 2, B_HEAD)
            on = attend(k, v, q)
        x3 = _mix_out(False, alpha, [on], x2, mod1, tiles, tm, b_wo, gn, ln[1][0])
        y_out = _mlp(alpha, x3, mod1, tiles_mlp, tm_mlp, ffn1[1], ffn2[1], ln[1][1])
        return (y_out.reshape(b, seq_len, d), k_out, v.reshape(b, seq_len, heads, 2 * B_HEAD),
                s_fin[None], last.reshape(1, b, d))

    def attend_prompt(kb, vt, qt):
        on = _diff_attn_prompt(qt, kb.reshape(bp, seq, d), vt, b_lam_q[0], b_lam_k[0], g_head.reshape(LANES, 1),
                               lam_init, min(1024, seq))
        return on.reshape(bp * seq, d)

    def attend_sample(k, v, q):
        cache_kt = jnp.transpose(cache_k, (0, 2, 3, 4, 1)).reshape(pool, d, PAGE_SIZE)
        on = _diff_attn_decode(q, k, v.reshape(bd, heads, LANES), cache_kt,
                               cache_v.reshape(pool, PAGE_SIZE * heads, LANES), page_table,
                               b_lam_q[0], b_lam_k[0], g_head, lam_init, min(8, page_table.shape[1]))
        return on.reshape(bd, d)

    zeros_shift = jnp.zeros((bp, d), F32)
    zeros_wkv = jnp.zeros((bp, d // A_HEAD, A_HEAD, A_HEAD), F32)
    y_p, k_p, v_p, wkv_p, shift_p = trunk(x_prompt, slice(0, bp), zeros_shift, zeros_wkv, seq,
                                          min(512, seq), min(1024, seq), min(128, seq), attend_prompt)
    y_s, k_s, v_s, wkv_s, shift_s = trunk(x_sample, slice(bp, bp + bd), state_shift[0], state_wkv[0], 1,
                                          bd, bd, 1, attend_sample)
    return (y_p, y_s, k_p, v_p, wkv_p, shift_p, k_s, v_s, wkv_s, shift_s)
```

```python
import functools
import math

import jax
import jax.numpy as jnp
from jax import lax
from jax.experimental import pallas as pl
from jax.experimental.pallas import tpu as pltpu

F32 = jnp.float32
BF16 = jnp.bfloat16

LANES = 128
A_HEAD = 64
B_HEAD = 64
PAGE_SIZE = 128
GN_EPS = 64e-5
SUBLN_EPS = 1e-5
LN_EPS = 1e-5
NEG = -1e30
LOG2E = math.log2(math.e)
VMEM_LIMIT = 56 * 2**20


def _params(sem):
    return pltpu.CompilerParams(dimension_semantics=sem, vmem_limit_bytes=VMEM_LIMIT)


def _dot(a, b):
    return jnp.dot(a, b, preferred_element_type=F32)


def _split(x):
    hi = x.astype(BF16)
    lo = (x - hi.astype(F32)).astype(BF16)
    return hi, lo


def _sigmoid(z):
    return 1.0 / (1.0 + jnp.exp(-z))


def _layer_norm(z, g, b):
    mu = jnp.mean(z, -1, keepdims=True)
    d = z - mu
    var = jnp.mean(d * d, -1, keepdims=True)
    return d * lax.rsqrt(var + LN_EPS) * g + b


def _pair_ones():
    r = lax.broadcasted_iota(jnp.int32, (LANES, LANES), 0) >> 6
    c = lax.broadcasted_iota(jnp.int32, (LANES, LANES), 1) >> 6
    return jnp.where(r == c, 1.0, 0.0).astype(BF16)


def _head_sum(x, ones):
    m, d = x.shape
    cols = d // LANES
    stacked = jnp.concatenate([x[:, c * LANES:(c + 1) * LANES] for c in range(cols)], axis=0)
    hi, lo = _split(stacked)
    s = _dot(hi, ones) + _dot(lo, ones)
    return jnp.concatenate([s[c * m:(c + 1) * m] for c in range(cols)], axis=1)


def _mod_spec(mod, tiles_per_batch):
    return pl.BlockSpec((1, mod.shape[1], mod.shape[2]), lambda i, *_: (i // tiles_per_batch, 0, 0))


def _full(a):
    nd = a.ndim
    return pl.BlockSpec(a.shape, lambda *_: (0,) * nd)


def _ada_kernel(c_ref, w_ref, b_ref, o_ref):
    c = c_ref[...]
    s = c * _sigmoid(c)
    o_ref[...] = _dot(s.astype(BF16), w_ref[0].astype(BF16)) + b_ref[0]


def _ada(c, w, b, layer):
    m, d = c.shape
    n = w.shape[2]
    tn = 1024
    return pl.pallas_call(
        _ada_kernel,
        out_shape=jax.ShapeDtypeStruct((m, n), F32),
        grid=(n // tn,),
        in_specs=[pl.BlockSpec((m, d), lambda j: (0, 0)),
                  pl.BlockSpec((1, d, tn), lambda j: (layer, 0, j)),
                  pl.BlockSpec((1, 1, tn), lambda j: (layer, 0, j))],
        out_specs=pl.BlockSpec((m, tn), lambda j: (0, j)),
        compiler_params=_params(("arbitrary",)),
        name="ada_proj",
    )(c, w, b)


def _rwkv_proj_kernel(single_step, tiles_per_batch, d,
                      x_ref, xp_ref, mod_ref, sh0_ref, mu_ref, wrkv_ref, w1_ref, w2_ref, a1_ref, a2_ref,
                      g1_ref, g2_ref, vec_ref,
                      r_o, w_o, k_o, v_o, a_o, b_o, g_o, bonus_o, last_o):
    mod = mod_ref[0]
    sh = mod[:, 0:d]
    sc = mod[:, d:2 * d]
    x = x_ref[...]
    h = x * (1.0 + sc) + sh
    if single_step:
        h_prev = sh0_ref[0]
        last_o[0] = h
    else:
        i = pl.program_id(0)
        prev_row = xp_ref[7:8, :] * (1.0 + sc) + sh
        first = jnp.where(i % tiles_per_batch == 0, sh0_ref[0], prev_row)
        rolled = pltpu.roll(h, 1, 0)
        rid = lax.broadcasted_iota(jnp.int32, h.shape, 0)
        h_prev = jnp.where(rid == 0, first, rolled)
        last_o[0] = h[h.shape[0] - 1:, :]
    xx = h_prev - h

    def mix(p):
        return (h + xx * mu_ref[p:p + 1, :]).astype(BF16)

    r = _dot(mix(0), wrkv_ref[0])
    k = _dot(mix(1), wrkv_ref[1])
    v = _dot(mix(2), wrkv_ref[2])
    lw = _dot(jnp.tanh(_dot(mix(3), w1_ref[...])).astype(BF16), w2_ref[...])
    la = _dot(_dot(mix(4), a1_ref[...]).astype(BF16), a2_ref[...])
    g = _dot(_sigmoid(_dot(mix(5), g1_ref[...])).astype(BF16), g2_ref[...])

    w0 = vec_ref[0:1, :]
    a0 = vec_ref[1:2, :]
    k_k = vec_ref[2:3, :]
    k_a = vec_ref[3:4, :]
    r_k = vec_ref[4:5, :]

    z = -(w0 + lw)
    softplus = jnp.maximum(z, 0.0) + jnp.log(1.0 + jnp.exp(-jnp.abs(z)))
    w_log = -softplus - 0.5
    log_decay = -jnp.exp(w_log)
    a_gate = _sigmoid(a0 + la)
    ones = _pair_ones()
    kkf = k * k_k
    norm = jnp.sqrt(_head_sum(kkf * kkf, ones))
    kk = kkf / jnp.maximum(norm, 1e-12)
    k_mod = k * (1.0 + (a_gate - 1.0) * k_a)
    bonus = _head_sum(r * k_mod * r_k, ones) * v

    w_o[...] = jnp.exp(log_decay) if single_step else log_decay
    for ref, val in ((r_o, r), (k_o, k_mod), (v_o, v), (a_o, -kk), (b_o, kk * a_gate), (g_o, g), (bonus_o, bonus)):
        ref[...] = val.astype(ref.dtype)


def _rwkv_proj(x2d, mod, shift0, seq_len, tm, mu, wrkv, w1, w2, a1, a2, g1, g2, vec):
    n, d = x2d.shape
    single = seq_len == 1
    tiles_per_batch = 1 if single else seq_len // tm
    nt = n // tm
    nb = mod.shape[0]
    last_rows = tm if single else 1
    row = pl.BlockSpec((tm, d), lambda i: (i, 0))
    if single:
        xp_spec = pl.BlockSpec((8, d), lambda i: (0, 0))
        sh0_spec = pl.BlockSpec((1, tm, d), lambda i: (i, 0, 0))
    else:
        xp_spec = pl.BlockSpec((8, d), lambda i: (jnp.maximum(i * (tm // 8) - 1, 0), 0))
        sh0_spec = pl.BlockSpec((1, 1, d), lambda i: (i // tiles_per_batch, 0, 0))
    sds = lambda dtype: jax.ShapeDtypeStruct((n, d), dtype)
    seq_dtype = F32 if single else BF16
    outs = pl.pallas_call(
        functools.partial(_rwkv_proj_kernel, single, tiles_per_batch, d),
        out_shape=[sds(seq_dtype), sds(F32)] + [sds(seq_dtype)] * 6 + [jax.ShapeDtypeStruct((nb, last_rows, d), F32)],
        grid=(nt,),
        in_specs=[row, xp_spec, _mod_spec(mod, tiles_per_batch), sh0_spec, _full(mu), _full(wrkv), _full(w1),
                  _full(w2), _full(a1), _full(a2), _full(g1), _full(g2), _full(vec)],
        out_specs=[row] * 8 + [pl.BlockSpec((1, last_rows, d), lambda i: (i // tiles_per_batch, 0, 0))],
        compiler_params=_params(("arbitrary",)),
        name="rwkv_proj",
    )(x2d, x2d, mod, shift0, mu, wrkv, w1, w2, a1, a2, g1, g2, vec)
    return outs


def _wkv_kernel(tc, r_ref, w_ref, k_ref, v_ref, a_ref, b_ref, s0_ref, y_ref, s_ref):
    @pl.when(pl.program_id(1) == 0)
    def _():
        s_ref[...] = s0_ref[...]

    groups, pairs = s_ref.shape[0], s_ref.shape[1]
    ones = _pair_ones()
    rid = lax.broadcasted_iota(jnp.int32, (A_HEAD, LANES), 0)
    cid = lax.broadcasted_iota(jnp.int32, (A_HEAD, LANES), 1) & (A_HEAD - 1)
    diag = rid == cid
    idx = [(g, p) for g in range(groups) for p in range(pairs)]
    n = len(idx)
    sets = [list(range(n))]

    rb = min(8, tc)
    sub = lax.broadcasted_iota(jnp.int32, (rb, LANES), 0)

    def block(tb, carry):
        base = pl.multiple_of(tb * rb, rb)

        def tile(ref, g, p):
            return ref[g, pl.ds(base, rb), p * LANES:(p + 1) * LANES]

        r_t, w_t, k_t, v_t, a_t, b_t = ([tile(ref, g, p) for g, p in idx]
                                        for ref in (r_ref, w_ref, k_ref, v_ref, a_ref, b_ref))
        states = [s_ref[g, p] for g, p in idx]
        y_tiles = [jnp.zeros((rb, LANES), F32) for _ in idx]
        for i in range(rb):
            row = lambda tiles, j: tiles[j][i:i + 1, :]
            for members in sets:
                sa_in = jnp.concatenate([(states[j] * row(a_t, j)).astype(BF16) for j in members], axis=0)
                vd = jnp.concatenate([jnp.where(diag, row(v_t, j), 0.0).astype(BF16) for j in members], axis=0)
                sa = _dot(sa_in, ones)
                vb = _dot(vd, ones)
                for q, j in enumerate(members):
                    sl = slice(q * A_HEAD, (q + 1) * A_HEAD)
                    states[j] = states[j] * row(w_t, j) + sa[sl] * row(b_t, j) + vb[sl] * row(k_t, j)
                y_in = jnp.concatenate([(states[j] * row(r_t, j)).astype(BF16) for j in members], axis=0)
                yb = _dot(y_in, ones)
                for q, j in enumerate(members):
                    sl = slice(q * A_HEAD, (q + 1) * A_HEAD)
                    y_row = jnp.sum(jnp.where(diag, yb[sl], 0.0), axis=0, keepdims=True)
                    y_tiles[j] = jnp.where(sub == i, y_row, y_tiles[j])
        for j, (g, p) in enumerate(idx):
            s_ref[g, p] = states[j]
            y_ref[g, pl.ds(base, rb), p * LANES:(p + 1) * LANES] = y_tiles[j]
        return carry

    lax.fori_loop(0, tc // rb, block, 0)


def _wkv(seqs, s0, tc, group):
    b, t, d = seqs[0].shape
    pairs = s0.shape[1]
    seq_spec = pl.BlockSpec((group, tc, d), lambda i, j: (i, j, 0))
    st_spec = pl.BlockSpec((group, pairs, A_HEAD, LANES), lambda i, j: (i, 0, 0, 0))
    return pl.pallas_call(
        functools.partial(_wkv_kernel, tc),
        out_shape=[jax.ShapeDtypeStruct((b, t, d), F32), jax.ShapeDtypeStruct(s0.shape, F32)],
        grid=(b // group, t // tc),
        in_specs=[seq_spec] * 6 + [st_spec],
        out_specs=[seq_spec, st_spec],
        compiler_params=_params(("arbitrary", "arbitrary")),
        name="wkv7_scan",
    )(*seqs, s0)


WKV_CHUNK = 64


def _dot_nt(a, b):
    return lax.dot_general(a, b, (((1,), (1,)), ((), ())), preferred_element_type=F32)


def _dot_tn(a, b):
    return lax.dot_general(a, b, (((0,), (0,)), ((), ())), preferred_element_type=F32)


def _wkv_chunk_kernel(r_ref, lw_ref, k_ref, v_ref, a_ref, b_ref, s0_ref, y_ref, s_ref):
    @pl.when(pl.program_id(1) == 0)
    def _():
        s_ref[...] = s0_ref[...]

    groups, pairs = s_ref.shape[0], s_ref.shape[1]
    c = r_ref.shape[1]
    iota = lambda shape, dim: lax.broadcasted_iota(jnp.int32, shape, dim)
    head0 = iota((c, LANES), 1) < A_HEAD
    tri = jnp.where(iota((c, c), 1) <= iota((c, c), 0), 1.0, 0.0).astype(BF16)
    rows4 = iota((4 * c, LANES), 0)
    t_idx = rows4 & (c - 1)
    s_idx = iota((4 * c, LANES), 1) & (c - 1)
    keep = s_idx < t_idx + jnp.where(rows4 < 2 * c, 0, 1)
    same_head = (iota((LANES, LANES), 0) >> 6) == (iota((LANES, LANES), 1) >> 6)
    same_block = (iota((LANES, LANES), 0) >> 3) == (iota((LANES, LANES), 1) >> 3)
    eye = jnp.where(iota((LANES, LANES), 0) == iota((LANES, LANES), 1), 1.0, 0.0)

    def by_head(x):
        return jnp.concatenate([jnp.where(head0, x, 0.0), jnp.where(head0, 0.0, x)], axis=0)

    units = [(g, p) for g in range(groups) for p in range(pairs)]
    each = lambda f, *lists: [f(*args) for args in zip(*lists)]
    bf = lambda xs: [x.astype(BF16) for x in xs]
    swap = lambda x: pltpu.roll(x, A_HEAD, 1)

    def load(ref):
        return [ref[g, :, p * LANES:(p + 1) * LANES].astype(F32) for g, p in units]

    r, lw, k, v, a, b = (load(ref) for ref in (r_ref, lw_ref, k_ref, v_ref, a_ref, b_ref))
    s_old = [s_ref[g, p] for g, p in units]

    def cumulative(x):
        hi, lo = _split(x)
        return _dot(tri, hi) + _dot(tri, lo)

    cum = each(cumulative, lw)
    end = [x[c - 1:c, :] for x in cum]
    inv = [jnp.exp(-x) for x in cum]
    rest = each(lambda e, x: jnp.exp(e - x), end, cum)
    at = each(lambda a_, x, l_: a_ * jnp.exp(x - l_), a, cum, lw)
    rt = each(lambda r_, x: r_ * jnp.exp(x), r, cum)
    bt = each(jnp.multiply, b, inv)
    kt = each(jnp.multiply, k, inv)
    bb = bf(each(jnp.multiply, b, rest))
    kb = bf(each(jnp.multiply, k, rest))
    at_b, rt_b, v_b = bf(at), bf(rt), bf(v)
    lhs = each(lambda x, y: jnp.concatenate([by_head(x), by_head(y)], axis=0).astype(BF16), at, rt)
    rhs = each(lambda x, y: jnp.concatenate([x, y], axis=0).astype(BF16), bt, kt)
    aa = each(lambda x, y: jnp.where(keep, _dot_nt(x, y), 0.0), lhs, rhs)
    a0, a1, r0, r1 = ([x[i * c:(i + 1) * c] for x in aa] for i in range(4))
    n = each(lambda x, y: jnp.concatenate([jnp.where(head0, x, 0.0), jnp.where(head0, 0.0, swap(y))], axis=0), a0, a1)
    a_ak = bf(each(lambda x, y: jnp.where(head0, swap(x), y), a0, a1))
    a_rb = bf(each(lambda x, y: jnp.where(head0, x, swap(y)), r0, r1))
    a_rk = bf(each(lambda x, y: jnp.where(head0, swap(x), y), r0, r1))
    mm = lambda p_, q_: _dot(p_.astype(BF16), q_.astype(BF16))

    def neumann8(z):
        out = [eye + z_ for z_ in z]
        for _ in range(2):
            z = each(mm, z, z)
            out = each(lambda o_, z_: o_ + mm(o_, z_), out, z)
        return out

    d_in = [jnp.where(same_block, n_, 0.0) for n_ in n]
    d_inv = neumann8(d_in)
    m_mat = each(lambda di_, n_, d_: mm(di_, n_ - d_), d_inv, n, d_in)
    inv_t = each(mm, neumann8(m_mat), d_inv)
    v_heads = bf(each(by_head, v))
    s_b = bf(s_old)
    rhs0 = each(lambda at_, s_, ak_, vh_: _dot_nt(at_, s_) + _dot(ak_, vh_), at_b, s_b, a_ak, v_heads)
    uu = each(lambda t_, x_: _dot(t_.astype(BF16), by_head(x_).astype(BF16)), inv_t, rhs0)
    u = [x_[0:c] + x_[c:2 * c] for x_ in uu]
    u_heads = bf(each(by_head, u))
    y = each(lambda rt_, s_, rb_, uh_, rk_, vh_: _dot_nt(rt_, s_) + _dot(rb_, uh_) + _dot(rk_, vh_),
             rt_b, s_b, a_rb, u_heads, a_rk, v_heads)
    upd = each(lambda u_, bb_, v_, kb_: _dot_tn(u_, bb_) + _dot_tn(v_, kb_), bf(u), bb, v_b, kb)
    s_new = each(lambda s_, e, up: s_ * jnp.exp(e) + jnp.where(same_head, up, 0.0), s_old, end, upd)
    for (g, p), y_, s_ in zip(units, y, s_new):
        y_ref[g, :, p * LANES:(p + 1) * LANES] = y_
        s_ref[g, p] = s_


def _wkv_chunked(seqs, s0, group):
    b, t, d = seqs[0].shape
    pairs = s0.shape[1]
    seq_spec = pl.BlockSpec((group, WKV_CHUNK, d), lambda i, j: (i, j, 0))
    st_spec = pl.BlockSpec((group, pairs, LANES, LANES), lambda i, j: (i, 0, 0, 0))
    return pl.pallas_call(
        _wkv_chunk_kernel,
        out_shape=[jax.ShapeDtypeStruct((b, t, d), F32), jax.ShapeDtypeStruct(s0.shape, F32)],
        grid=(b // group, t // WKV_CHUNK),
        in_specs=[seq_spec] * 6 + [st_spec],
        out_specs=[seq_spec, st_spec],
        compiler_params=_params(("arbitrary", "arbitrary")),
        name="wkv7_chunked",
    )(*seqs, s0)


def _pair_block_diag(s):
    b, h, n, _ = s.shape
    s = s.reshape(b, h // 2, 2, n, n)
    z = jnp.zeros_like(s[:, :, 0])
    top = jnp.concatenate([s[:, :, 0], z], axis=-1)
    bot = jnp.concatenate([z, s[:, :, 1]], axis=-1)
    return jnp.concatenate([top, bot], axis=-2)


def _pair_diag_blocks(s):
    b, p, _, _ = s.shape
    n = A_HEAD
    return jnp.stack([s[:, :, 0:n, 0:n], s[:, :, n:, n:]], axis=2).reshape(b, 2 * p, n, n)


def _mix_out_kernel(rwkv, alpha, d, *refs):
    if rwkv:
        y_ref, g_ref, bonus_ref, x_ref, mod_ref, wo_ref, gn_ref, ln_ref, o_ref = refs
        y = y_ref[...]
        ones = _pair_ones()
        mu_y = _head_sum(y, ones) * (1.0 / A_HEAD)
        dy = y - mu_y
        var_y = _head_sum(dy * dy, ones) * (1.0 / A_HEAD)
        yn = dy * lax.rsqrt(var_y + GN_EPS) * gn_ref[0:1, :] + gn_ref[1:2, :]
        pre = ((yn + bonus_ref[...].astype(F32)) * g_ref[...].astype(F32)).astype(BF16)
    else:
        on_ref, x_ref, mod_ref, wo_ref, ln_ref, o_ref = refs
        pre = on_ref[...].astype(BF16)
    mix = _dot(pre, wo_ref[...])
    gt = mod_ref[0][:, 2 * d:3 * d]
    z = alpha * x_ref[...] + (1.0 + gt) * mix
    o_ref[...] = _layer_norm(z, ln_ref[0:1, :], ln_ref[1:2, :])


def _mix_out(rwkv, alpha, ins, x2d, mod, tiles_per_batch, tm, wo, gn, ln):
    n, d = x2d.shape
    row = pl.BlockSpec((tm, d), lambda i: (i, 0))
    extra = [gn] if rwkv else []
    args = list(ins) + [x2d, mod, wo] + extra + [ln]
    specs = [row] * (len(ins) + 1) + [_mod_spec(mod, tiles_per_batch), _full(wo)] + [_full(e) for e in extra] + [_full(ln)]
    return pl.pallas_call(
        functools.partial(_mix_out_kernel, rwkv, alpha, d),
        out_shape=jax.ShapeDtypeStruct((n, d), F32),
        grid=(n // tm,),
        in_specs=specs,
        out_specs=row,
        compiler_params=_params(("arbitrary",)),
        name="rwkv_out_ln" if rwkv else "attn_out_ln",
    )(*args)


def _mlp_kernel(alpha, d, x_ref, mod_ref, w1_ref, w2_ref, ln_ref, o_ref, hf_ref, acc_ref):
    j = pl.program_id(1)
    mod = mod_ref[0]

    @pl.when(j == 0)
    def _():
        hf_ref[...] = (x_ref[...] * (1.0 + mod[:, 4 * d:5 * d]) + mod[:, 3 * d:4 * d]).astype(BF16)
        acc_ref[...] = jnp.zeros_like(acc_ref)

    h1 = jnp.maximum(_dot(hf_ref[...], w1_ref[...]), 0.0)
    acc_ref[...] += _dot((h1 * h1).astype(BF16), w2_ref[...])

    @pl.when(j == pl.num_programs(1) - 1)
    def _():
        z = alpha * x_ref[...] + (1.0 + mod[:, 5 * d:6 * d]) * acc_ref[...]
        o_ref[...] = _layer_norm(z, ln_ref[0:1, :], ln_ref[1:2, :])


def _mlp(alpha, x2d, mod, tiles_per_batch, tm, w1, w2, ln):
    n, d = x2d.shape
    dff = w1.shape[1]
    tf = 1024
    row = pl.BlockSpec((tm, d), lambda i, j: (i, 0))
    return pl.pallas_call(
        functools.partial(_mlp_kernel, alpha, d),
        out_shape=jax.ShapeDtypeStruct((n, d), F32),
        grid=(n // tm, dff // tf),
        in_specs=[row, _mod_spec(mod, tiles_per_batch),
                  pl.BlockSpec((d, tf), lambda i, j: (0, j)),
                  pl.BlockSpec((tf, d), lambda i, j: (j, 0)),
                  _full(ln)],
        out_specs=row,
        scratch_shapes=[pltpu.VMEM((tm, d), BF16), pltpu.VMEM((tm, d), F32)],
        compiler_params=_params(("arbitrary", "arbitrary")),
        name="sq_relu_mlp_ln",
    )(x2d, mod, w1, w2, ln)


def _qkv_kernel(d, scale, transposed, x_ref, kvmod_ref, mod_ref, wk_ref, wv_ref, wq_ref, k_o, v_o, *extra):
    x = x_ref[...]
    kvmod = kvmod_ref[0]
    hkv = (x * (1.0 + kvmod[:, d:2 * d]) + kvmod[:, 0:d]).astype(BF16)
    k = _dot(hkv, wk_ref[...])
    v = _dot(hkv, wv_ref[...])
    v_o[...] = v
    mod = mod_ref[0]
    h = (x * (1.0 + mod[:, d:2 * d]) + mod[:, 0:d]).astype(BF16)
    q = _dot(h, wq_ref[...]) * scale
    if transposed:
        kb_o, vt_o, qt_o = extra
        k_o[0] = k.T
        kb_o[...] = k.astype(BF16)
        vt_o[0] = v.T.astype(BF16)
        qt_o[0] = q.T.astype(BF16)
    else:
        (q_o,) = extra
        k_o[...] = k
        q_o[...] = q


def _qkv(x2d, kvmod, mod, tiles_per_batch, tm, wk, wv, wq, scale, transposed):
    n, d = x2d.shape
    row = pl.BlockSpec((tm, d), lambda i: (i, 0))
    row_sds = jax.ShapeDtypeStruct((n, d), F32)
    if transposed:
        nb = n // (tiles_per_batch * tm)
        col = pl.BlockSpec((1, d, tm), lambda i: (i // tiles_per_batch, 0, i % tiles_per_batch))
        col_shape = (nb, d, tiles_per_batch * tm)
        t_sds = jax.ShapeDtypeStruct(col_shape, BF16)
        out_shape = [jax.ShapeDtypeStruct(col_shape, F32), row_sds, jax.ShapeDtypeStruct((n, d), BF16), t_sds, t_sds]
        out_specs = [col, row, row, col, col]
    else:
        out_shape = [row_sds] * 3
        out_specs = [row] * 3
    return pl.pallas_call(
        functools.partial(_qkv_kernel, d, scale, transposed),
        out_shape=out_shape,
        grid=(n // tm,),
        in_specs=[row, _mod_spec(kvmod, tiles_per_batch), _mod_spec(mod, tiles_per_batch), _full(wk), _full(wv), _full(wq)],
        out_specs=out_specs,
        compiler_params=_params(("arbitrary",)),
        name="qkv_proj",
    )(x2d, kvmod, mod, wk, wv, wq)


def _diff_lambda(lq_ref, lk_ref, lam_init):
    lq = lq_ref[...]
    lk = lk_ref[...]
    dots = jnp.sum(lq * lk, -1, keepdims=True)
    return jnp.exp(dots[0:1, :]) - jnp.exp(dots[1:2, :]) + lam_init


DENOM_ROWS = 16


def _attn_kernel(tq, lam_init, qi_tab, ki_tab, qt_ref, k_ref, vt_ref, lq_ref, lk_ref, g_ref, o_ref,
                 qs_ref, m_ref, acc_ref):
    t = pl.program_id(2)
    qi = qi_tab[t]
    ki = ki_tab[t]

    heads_here = qs_ref.shape[0]
    head = lambda e: slice(e * LANES, (e + 1) * LANES)

    @pl.when(ki == 0)
    def _():
        for e in range(heads_here):
            qt = qt_ref[0, head(e), :]
            row = lax.broadcasted_iota(jnp.int32, qt.shape, 0)
            qs_ref[e, :, 0:tq] = jnp.where(row < B_HEAD, qt, jnp.zeros_like(qt))
            qs_ref[e, :, tq:2 * tq] = jnp.where(row >= B_HEAD, qt, jnp.zeros_like(qt))
        m_ref[...] = jnp.full(m_ref.shape, NEG, F32)
        acc_ref[...] = jnp.zeros_like(acc_ref)

    def update(diagonal):
        for e in range(heads_here):
            s = _dot(k_ref[0, :, head(e)], qs_ref[e])
            if diagonal:
                kpos = lax.broadcasted_iota(jnp.int32, s.shape, 0)
                qpos = lax.broadcasted_iota(jnp.int32, s.shape, 1) & (tq - 1)
                s = jnp.where(kpos <= qpos, s, NEG)
            m_prev = m_ref[e]
            m_new = jnp.maximum(m_prev, jnp.max(s, 0, keepdims=True))
            alpha = jnp.exp2(m_prev - m_new)
            p = jnp.exp2(s - m_new).astype(BF16)
            vt_ones = jnp.concatenate([vt_ref[0, head(e), :], jnp.ones((DENOM_ROWS, vt_ref.shape[2]), BF16)], axis=0)
            acc_ref[e] = alpha * acc_ref[e] + _dot(vt_ones, p)
            m_ref[e] = m_new

    @pl.when(ki < qi)
    def _():
        update(False)

    @pl.when(ki == qi)
    def _():
        update(True)
        lam = _diff_lambda(lq_ref, lk_ref, lam_init)
        for e in range(heads_here):
            o_all = acc_ref[e, 0:LANES, :] * (1.0 / acc_ref[e, LANES:LANES + 1, :])
            o = o_all[:, 0:tq] - lam * o_all[:, tq:2 * tq]
            on = o * lax.rsqrt(jnp.mean(o * o, 0, keepdims=True) + SUBLN_EPS) * g_ref[...] * (1.0 - lam_init)
            o_ref[0, :, head(e)] = on.T.astype(o_ref.dtype)


def _diff_attn_prompt(qt, k, vt, lam_q, lam_k, g_col, lam_init, tq):
    b, s, d = k.shape
    heads = d // LANES
    nq = s // tq
    pairs = [(qi, ki) for qi in range(nq) for ki in range(qi + 1)]
    qi_tab = jnp.asarray([p[0] for p in pairs], jnp.int32)
    ki_tab = jnp.asarray([p[1] for p in pairs], jnp.int32)
    hp = 2 if heads % 2 == 0 else 1
    width = hp * LANES
    qt_spec = pl.BlockSpec((1, width, tq), lambda bi, h, t, qt_, kt_: (bi, h, qt_[t]))
    vt_spec = pl.BlockSpec((1, width, tq), lambda bi, h, t, qt_, kt_: (bi, h, kt_[t]))
    k_spec = pl.BlockSpec((1, tq, width), lambda bi, h, t, qt_, kt_: (bi, kt_[t], h))
    o_spec = pl.BlockSpec((1, tq, width), lambda bi, h, t, qt_, kt_: (bi, qt_[t], h))
    small = lambda a: pl.BlockSpec(a.shape, lambda *_: (0,) * a.ndim)
    grid_spec = pltpu.PrefetchScalarGridSpec(
        num_scalar_prefetch=2,
        grid=(b, heads // hp, len(pairs)),
        in_specs=[qt_spec, k_spec, vt_spec, small(lam_q), small(lam_k), small(g_col)],
        out_specs=o_spec,
        scratch_shapes=[pltpu.VMEM((hp, LANES, 2 * tq), BF16), pltpu.VMEM((hp, 1, 2 * tq), F32),
                        pltpu.VMEM((hp, LANES + DENOM_ROWS, 2 * tq), F32)],
    )
    return pl.pallas_call(
        functools.partial(_attn_kernel, tq, lam_init),
        out_shape=jax.ShapeDtypeStruct((b, s, d), BF16),
        grid_spec=grid_spec,
        compiler_params=_params(("arbitrary",) * 3),
        name="diff_attn_prompt",
    )(qi_tab, ki_tab, qt, k, vt, lam_q, lam_k, g_col)


def _decode_kernel(pp, lam_init, heads, pt_ref, qcol_ref, qrow_ref, kn_ref, vn_ref, lq_ref, lk_ref, g_ref, *rest):
    k_refs = rest[:pp]
    v_refs = rest[pp:2 * pp]
    o_ref, qb_ref, m_ref, l_ref, acc_ref = rest[2 * pp:]
    step = pl.program_id(1)
    rows = 2 * heads
    d = qb_ref.shape[0]

    @pl.when(step == 0)
    def _():
        qb_ref[...] = jnp.broadcast_to(qcol_ref[0], qb_ref.shape)
        m_ref[...] = jnp.full(m_ref.shape, NEG, F32)
        l_ref[...] = jnp.zeros_like(l_ref)
        acc_ref[...] = jnp.zeros_like(acc_ref)

    r = lax.broadcasted_iota(jnp.int32, (rows, d), 0)
    c = lax.broadcasted_iota(jnp.int32, (rows, d), 1)
    sel = ((c >> 7) == (r & (heads - 1))) & (((c >> 6) & 1) == (r // heads))
    sel_b = jnp.where(sel, 1.0, 0.0).astype(BF16)
    row_head = lax.broadcasted_iota(jnp.int32, (rows, LANES), 0) & (heads - 1)

    s = jnp.concatenate([_dot(sel_b, (k_refs[i][0] * qb_ref[...]).astype(BF16)) for i in range(pp)], axis=1)
    m_prev = m_ref[...]
    m_new = jnp.maximum(m_prev, jnp.max(s, -1, keepdims=True))
    alpha = jnp.exp(m_prev - m_new)
    p = jnp.exp(s - m_new)
    l_ref[...] = alpha * l_ref[...] + jnp.sum(p, -1, keepdims=True)
    pb = p.astype(BF16)
    pv = jnp.zeros((rows, LANES), F32)
    for i in range(pp):
        pb_i = pb[:, i * PAGE_SIZE:(i + 1) * PAGE_SIZE]
        for h in range(heads):
            v_h = v_refs[i][0, pl.ds(h, PAGE_SIZE, stride=heads), :].astype(BF16)
            pv = pv + _dot(jnp.where(row_head == h, pb_i, jnp.zeros_like(pb_i)), v_h)
    acc_ref[...] = alpha * acc_ref[...] + pv
    m_ref[...] = m_new

    @pl.when(step == pl.num_programs(1) - 1)
    def _():
        s_new = jnp.sum(jnp.where(sel, qrow_ref[0] * kn_ref[0], 0.0), -1, keepdims=True)
        m_prev = m_ref[...]
        m_fin = jnp.maximum(m_prev, s_new)
        alpha = jnp.exp(m_prev - m_fin)
        p_new = jnp.exp(s_new - m_fin)
        l_fin = alpha * l_ref[...] + p_new
        o_all = (alpha * acc_ref[...] + p_new * vn_ref[0]) / l_fin
        lam = _diff_lambda(lq_ref, lk_ref, lam_init)
        o = o_all[0:heads] - lam * o_all[heads:rows]
        on = o * lax.rsqrt(jnp.mean(o * o, -1, keepdims=True) + SUBLN_EPS) * g_ref[...] * (1.0 - lam_init)
        o_ref[0] = on


def _diff_attn_decode(q, k_new, v_new, cache_kt, cache_v, page_table, lam_q, lam_k, g_row, lam_init, pp):
    bd, d = q.shape
    heads = d // LANES
    rows = 2 * heads
    n_pages = page_table.shape[1]
    at_b = lambda shape: pl.BlockSpec((1,) + shape, lambda b, s, pt: (b, 0, 0))

    def page_spec(shape, i):
        return pl.BlockSpec((1,) + shape, lambda b, s, pt: (pt[b, s * pp + i], 0, 0))

    small = lambda a: pl.BlockSpec(a.shape, lambda b, s, pt: (0,) * a.ndim)
    grid_spec = pltpu.PrefetchScalarGridSpec(
        num_scalar_prefetch=1,
        grid=(bd, n_pages // pp),
        in_specs=[at_b((d, 1)), at_b((1, d)), at_b((1, d)), at_b((rows, LANES)), small(lam_q), small(lam_k), small(g_row)]
                 + [page_spec((d, PAGE_SIZE), i) for i in range(pp)]
                 + [page_spec((PAGE_SIZE * heads, LANES), i) for i in range(pp)],
        out_specs=at_b((heads, LANES)),
        scratch_shapes=[pltpu.VMEM((d, PAGE_SIZE), F32), pltpu.VMEM((rows, 1), F32), pltpu.VMEM((rows, 1), F32),
                        pltpu.VMEM((rows, LANES), F32)],
    )
    v_rows = jnp.tile(v_new, (1, 2, 1))
    return pl.pallas_call(
        functools.partial(_decode_kernel, pp, lam_init, heads),
        out_shape=jax.ShapeDtypeStruct((bd, heads, LANES), F32),
        grid_spec=grid_spec,
        compiler_params=_params(("arbitrary", "arbitrary")),
        name="diff_attn_decode",
    )(page_table, q.reshape(bd, d, 1), q.reshape(bd, 1, d), k_new.reshape(bd, 1, d), v_rows, lam_q, lam_k, g_row,
      *([cache_kt] * pp), *([cache_v] * pp))


def _pad_to(a, axis, size):
    pad = [(0, 0)] * a.ndim
    pad[axis] = (0, size - a.shape[axis])
    return jnp.pad(a, pad)


def _pack_state(s):
    b, h, n, _ = s.shape
    return s.reshape(b, h // 2, 2, n, n).transpose(0, 1, 3, 2, 4).reshape(b, h // 2, n, 2 * n)


def _unpack_state(s):
    b, p, n, _ = s.shape
    return s.reshape(b, p, n, 2, n).transpose(0, 1, 3, 2, 4).reshape(b, 2 * p, n, n)


def kernel(x_prompt, x_sample, cache_k, cache_v, state_wkv, state_shift, page_table, c_prompt, c_sample, ln_g, ln_b, ada_w, ada_b, ffn_w1, ffn_w2, a_mu, a_w_rkv, a_w_o, a_w0, a_w1, a_w2, a_a0, a_a1, a_a2, a_g1, a_g2, a_k_k, a_k_a, a_r_k, a_gn_g, a_gn_b, kv_ada_w, kv_ada_b, kv_w_k, kv_w_v, b_w_q, b_w_o, b_lam_q, b_lam_k, b_subln_g):
    bp, seq, d = x_prompt.shape
    bd, dec_seq, _ = x_sample.shape
    depth = ada_w.shape[0]
    n_a = a_mu.shape[0]
    assert depth == 2 and n_a == 1 and dec_seq == 1
    alpha = (2 * depth) ** 0.25
    heads = d // LANES
    pool = cache_k.shape[0]

    n_c = bp + bd
    c_all = _pad_to(jnp.concatenate([c_prompt, c_sample], axis=0), 0, -(-n_c // 8) * 8)
    mods = [_ada(c_all, ada_w, ada_b.reshape(depth, 1, -1), l) for l in range(depth)]
    kvmods = _ada(c_all, kv_ada_w[None], kv_ada_b.reshape(1, 1, -1), 0)

    wrkv = a_w_rkv[0].astype(BF16)
    lora = 128
    w1 = _pad_to(a_w1[0], 1, lora).astype(BF16)
    w2 = _pad_to(a_w2[0], 0, lora).astype(BF16)
    a1 = _pad_to(a_a1[0], 1, lora).astype(BF16)
    a2 = _pad_to(a_a2[0], 0, lora).astype(BF16)
    g1 = _pad_to(a_g1[0], 1, 2 * lora).astype(BF16)
    g2 = _pad_to(a_g2[0], 0, 2 * lora).astype(BF16)
    vec = _pad_to(jnp.stack([a_w0[0], a_a0[0], a_k_k[0], a_k_a[0], a_r_k[0].reshape(d)]), 0, 8)
    gn = jnp.stack([a_gn_g[0], a_gn_b[0]])
    a_wo = a_w_o[0].astype(BF16)
    ffn1 = ffn_w1.astype(BF16)
    ffn2 = ffn_w2.astype(BF16)
    wk = kv_w_k.astype(BF16)
    wv = kv_w_v.astype(BF16)
    wq = b_w_q[0].astype(BF16)
    b_wo = b_w_o[0].astype(BF16)
    g_head = b_subln_g[0].reshape(1, LANES)
    lam_init = 0.8 - 0.6 * math.exp(-0.3 * n_a)
    ln = [[jnp.stack([ln_g[l, s], ln_b[l, s]]) for s in range(2)] for l in range(depth)]
    attn_scale = B_HEAD ** -0.5

    def trunk(x, rows, shift0, wkv0, seq_len, tm, tm_mlp, tc, attend):
        b = x.shape[0]
        x2d = x.reshape(b * seq_len, d)
        tiles = max(seq_len // tm, 1)
        tiles_mlp = max(seq_len // tm_mlp, 1)
        if seq_len == 1:
            shape_mod = lambda m: m[rows].reshape(1, b, -1)
            sh0 = shift0.reshape(1, b, d)
        else:
            shape_mod = lambda m: m[rows].reshape(b, 1, -1)
            sh0 = shift0.reshape(b, 1, d)
        mod0, mod1, kvmod = shape_mod(mods[0]), shape_mod(mods[1]), shape_mod(kvmods)

        *seqs, g, bonus, last = _rwkv_proj(x2d, mod0, sh0, seq_len, tm, a_mu[0], wrkv, w1, w2, a1, a2, g1, g2, vec)
        seqs = [s.reshape(b, seq_len, d) for s in seqs]
        if seq_len == 1:
            y, s_fin = _wkv(seqs, _pack_state(wkv0), tc, 2)
            s_fin = _unpack_state(s_fin)
        else:
            y, s_fin = _wkv_chunked(seqs, _pair_block_diag(wkv0), 2)
            s_fin = _pair_diag_blocks(s_fin)
        x1 = _mix_out(True, alpha, [y.reshape(b * seq_len, d), g, bonus], x2d, mod0, tiles, tm, a_wo, gn, ln[0][0])
        x2 = _mlp(alpha, x1, mod0, tiles_mlp, tm_mlp, ffn1[0], ffn2[0], ln[0][1])
        if seq_len > 1:
            kt, v, kb, vt, qt = _qkv(x2, kvmod, mod1, tiles, tm, wk, wv, wq, attn_scale * LOG2E, True)
            k_out = kt.reshape(b, heads, 2, B_HEAD, seq_len).transpose(0, 4, 1, 2, 3)
            on = attend(kb, vt, qt)
        else:
            k, v, q = _qkv(x2, kvmod, mod1, tiles, tm, wk, wv, wq, attn_scale, False)
            k_out = k.reshape(b, seq_len, heads, 2, B_HEAD)
            on = attend(k, v, q)
        x3 = _mix_out(False, alpha, [on], x2, mod1, tiles, tm, b_wo, gn, ln[1][0])
        y_out = _mlp(alpha, x3, mod1, tiles_mlp, tm_mlp, ffn1[1], ffn2[1], ln[1][1])
        return (y_out.reshape(b, seq_len, d), k_out, v.reshape(b, seq_len, heads, 2 * B_HEAD),
                s_fin[None], last.reshape(1, b, d))

    def attend_prompt(kb, vt, qt):
        on = _diff_attn_prompt(qt, kb.reshape(bp, seq, d), vt, b_lam_q[0], b_lam_k[0], g_head.reshape(LANES, 1),
                               lam_init, min(1024, seq))
        return on.reshape(bp * seq, d)

    def attend_sample(k, v, q):
        cache_kt = jnp.transpose(cache_k, (0, 2, 3, 4, 1)).reshape(pool, d, PAGE_SIZE)
        on = _diff_attn_decode(q, k, v.reshape(bd, heads, LANES), cache_kt,
                               cache_v.reshape(pool, PAGE_SIZE * heads, LANES), page_table,
                               b_lam_q[0], b_lam_k[0], g_head, lam_init, min(8, page_table.shape[1]))
        return on.reshape(bd, d)

    zeros_shift = jnp.zeros((bp, d), F32)
    zeros_wkv = jnp.zeros((bp, d // A_HEAD, A_HEAD, A_HEAD), F32)
    y_p, k_p, v_p, wkv_p, shift_p = trunk(x_prompt, slice(0, bp), zeros_shift, zeros_wkv, seq,
                                          min(512, seq), min(1024, seq), min(128, seq), attend_prompt)
    y_s, k_s, v_s, wkv_s, shift_s = trunk(x_sample, slice(bp, bp + bd), state_shift[0], state_wkv[0], 1,
                                          bd, bd, 1, attend_sample)
    return (y_p, y_s, k_p, v_p, wkv_p, shift_p, k_s, v_s, wkv_s, shift_s)
```

```python
import functools
import math

import jax
import jax.numpy as jnp
from jax import lax
from jax.experimental import pallas as pl
from jax.experimental.pallas import tpu as pltpu

F32 = jnp.float32
BF16 = jnp.bfloat16

LANES = 128
A_HEAD = 64
B_HEAD = 64
PAGE_SIZE = 128
GN_EPS = 64e-5
SUBLN_EPS = 1e-5
LN_EPS = 1e-5
NEG = -1e30
LOG2E = math.log2(math.e)
VMEM_LIMIT = 56 * 2**20


def _params(sem):
    return pltpu.CompilerParams(dimension_semantics=sem, vmem_limit_bytes=VMEM_LIMIT)


def _dot(a, b):
    return jnp.dot(a, b, preferred_element_type=F32)


def _split(x):
    hi = x.astype(BF16)
    lo = (x - hi.astype(F32)).astype(BF16)
    return hi, lo


def _sigmoid(z):
    return 1.0 / (1.0 + jnp.exp(-z))


def _layer_norm(z, g, b):
    mu = jnp.mean(z, -1, keepdims=True)
    d = z - mu
    var = jnp.mean(d * d, -1, keepdims=True)
    return d * lax.rsqrt(var + LN_EPS) * g + b


def _pair_ones():
    r = lax.broadcasted_iota(jnp.int32, (LANES, LANES), 0) >> 6
    c = lax.broadcasted_iota(jnp.int32, (LANES, LANES), 1) >> 6
    return jnp.where(r == c, 1.0, 0.0).astype(BF16)


def _head_sum(x, ones):
    m, d = x.shape
    cols = d // LANES
    stacked = jnp.concatenate([x[:, c * LANES:(c + 1) * LANES] for c in range(cols)], axis=0)
    hi, lo = _split(stacked)
    s = _dot(hi, ones) + _dot(lo, ones)
    return jnp.concatenate([s[c * m:(c + 1) * m] for c in range(cols)], axis=1)


def _mod_spec(mod, tiles_per_batch):
    return pl.BlockSpec((1, mod.shape[1], mod.shape[2]), lambda i, *_: (i // tiles_per_batch, 0, 0))


def _full(a):
    nd = a.ndim
    return pl.BlockSpec(a.shape, lambda *_: (0,) * nd)


def _ada_kernel(c_ref, w_ref, b_ref, o_ref):
    c = c_ref[...]
    s = c * _sigmoid(c)
    o_ref[...] = _dot(s.astype(BF16), w_ref[0].astype(BF16)) + b_ref[0]


def _ada(c, w, b, layer):
    m, d = c.shape
    n = w.shape[2]
    tn = 1024
    return pl.pallas_call(
        _ada_kernel,
        out_shape=jax.ShapeDtypeStruct((m, n), F32),
        grid=(n // tn,),
        in_specs=[pl.BlockSpec((m, d), lambda j: (0, 0)),
                  pl.BlockSpec((1, d, tn), lambda j: (layer, 0, j)),
                  pl.BlockSpec((1, 1, tn), lambda j: (layer, 0, j))],
        out_specs=pl.BlockSpec((m, tn), lambda j: (0, j)),
        compiler_params=_params(("arbitrary",)),
        name="ada_proj",
    )(c, w, b)


def _rwkv_proj_kernel(single_step, tiles_per_batch, d,
                      x_ref, xp_ref, mod_ref, sh0_ref, mu_ref, wrkv_ref, w1_ref, w2_ref, a1_ref, a2_ref,
                      g1_ref, g2_ref, vec_ref,
                      r_o, w_o, k_o, v_o, a_o, b_o, g_o, bonus_o, last_o):
    mod = mod_ref[0]
    sh = mod[:, 0:d]
    sc = mod[:, d:2 * d]
    x = x_ref[...]
    h = x * (1.0 + sc) + sh
    if single_step:
        h_prev = sh0_ref[0]
        last_o[0] = h
    else:
        i = pl.program_id(0)
        prev_row = xp_ref[7:8, :] * (1.0 + sc) + sh
        first = jnp.where(i % tiles_per_batch == 0, sh0_ref[0], prev_row)
        rolled = pltpu.roll(h, 1, 0)
        rid = lax.broadcasted_iota(jnp.int32, h.shape, 0)
        h_prev = jnp.where(rid == 0, first, rolled)
        last_o[0] = h[h.shape[0] - 1:, :]
    xx = h_prev - h

    def mix(p):
        return (h + xx * mu_ref[p:p + 1, :]).astype(BF16)

    r = _dot(mix(0), wrkv_ref[0])
    k = _dot(mix(1), wrkv_ref[1])
    v = _dot(mix(2), wrkv_ref[2])
    lw = _dot(jnp.tanh(_dot(mix(3), w1_ref[...])).astype(BF16), w2_ref[...])
    la = _dot(_dot(mix(4), a1_ref[...]).astype(BF16), a2_ref[...])
    g = _dot(_sigmoid(_dot(mix(5), g1_ref[...])).astype(BF16), g2_ref[...])

    w0 = vec_ref[0:1, :]
    a0 = vec_ref[1:2, :]
    k_k = vec_ref[2:3, :]
    k_a = vec_ref[3:4, :]
    r_k = vec_ref[4:5, :]

    z = -(w0 + lw)
    softplus = jnp.maximum(z, 0.0) + jnp.log(1.0 + jnp.exp(-jnp.abs(z)))
    w_log = -softplus - 0.5
    log_decay = -jnp.exp(w_log)
    a_gate = _sigmoid(a0 + la)
    ones = _pair_ones()
    kkf = k * k_k
    norm = jnp.sqrt(_head_sum(kkf * kkf, ones))
    kk = kkf / jnp.maximum(norm, 1e-12)
    k_mod = k * (1.0 + (a_gate - 1.0) * k_a)
    bonus = _head_sum(r * k_mod * r_k, ones) * v

    w_o[...] = jnp.exp(log_decay) if single_step else log_decay
    for ref, val in ((r_o, r), (k_o, k_mod), (v_o, v), (a_o, -kk), (b_o, kk * a_gate), (g_o, g), (bonus_o, bonus)):
        ref[...] = val.astype(ref.dtype)


def _rwkv_proj(x2d, mod, shift0, seq_len, tm, mu, wrkv, w1, w2, a1, a2, g1, g2, vec):
    n, d = x2d.shape
    single = seq_len == 1
    tiles_per_batch = 1 if single else seq_len // tm
    nt = n // tm
    nb = mod.shape[0]
    last_rows = tm if single else 1
    row = pl.BlockSpec((tm, d), lambda i: (i, 0))
    if single:
        xp_spec = pl.BlockSpec((8, d), lambda i: (0, 0))
        sh0_spec = pl.BlockSpec((1, tm, d), lambda i: (i, 0, 0))
    else:
        xp_spec = pl.BlockSpec((8, d), lambda i: (jnp.maximum(i * (tm // 8) - 1, 0), 0))
        sh0_spec = pl.BlockSpec((1, 1, d), lambda i: (i // tiles_per_batch, 0, 0))
    sds = lambda dtype: jax.ShapeDtypeStruct((n, d), dtype)
    seq_dtype = F32 if single else BF16
    outs = pl.pallas_call(
        functools.partial(_rwkv_proj_kernel, single, tiles_per_batch, d),
        out_shape=[sds(seq_dtype), sds(F32)] + [sds(seq_dtype)] * 6 + [jax.ShapeDtypeStruct((nb, last_rows, d), F32)],
        grid=(nt,),
        in_specs=[row, xp_spec, _mod_spec(mod, tiles_per_batch), sh0_spec, _full(mu), _full(wrkv), _full(w1),
                  _full(w2), _full(a1), _full(a2), _full(g1), _full(g2), _full(vec)],
        out_specs=[row] * 8 + [pl.BlockSpec((1, last_rows, d), lambda i: (i // tiles_per_batch, 0, 0))],
        compiler_params=_params(("arbitrary",)),
        name="rwkv_proj",
    )(x2d, x2d, mod, shift0, mu, wrkv, w1, w2, a1, a2, g1, g2, vec)
    return outs


def _wkv_kernel(tc, r_ref, w_ref, k_ref, v_ref, a_ref, b_ref, s0_ref, y_ref, s_ref):
    @pl.when(pl.program_id(1) == 0)
    def _():
        s_ref[...] = s0_ref[...]

    groups, pairs = s_ref.shape[0], s_ref.shape[1]
    ones = _pair_ones()
    rid = lax.broadcasted_iota(jnp.int32, (A_HEAD, LANES), 0)
    cid = lax.broadcasted_iota(jnp.int32, (A_HEAD, LANES), 1) & (A_HEAD - 1)
    diag = rid == cid
    idx = [(g, p) for g in range(groups) for p in range(pairs)]
    n = len(idx)
    sets = [list(range(n))]

    rb = min(8, tc)
    sub = lax.broadcasted_iota(jnp.int32, (rb, LANES), 0)

    def block(tb, carry):
        base = pl.multiple_of(tb * rb, rb)

        def tile(ref, g, p):
            return ref[g, pl.ds(base, rb), p * LANES:(p + 1) * LANES]

        r_t, w_t, k_t, v_t, a_t, b_t = ([tile(ref, g, p) for g, p in idx]
                                        for ref in (r_ref, w_ref, k_ref, v_ref, a_ref, b_ref))
        states = [s_ref[g, p] for g, p in idx]
        y_tiles = [jnp.zeros((rb, LANES), F32) for _ in idx]
        for i in range(rb):
            row = lambda tiles, j: tiles[j][i:i + 1, :]
            for members in sets:
                sa_in = jnp.concatenate([(states[j] * row(a_t, j)).astype(BF16) for j in members], axis=0)
                vd = jnp.concatenate([jnp.where(diag, row(v_t, j), 0.0).astype(BF16) for j in members], axis=0)
                sa = _dot(sa_in, ones)
                vb = _dot(vd, ones)
                for q, j in enumerate(members):
                    sl = slice(q * A_HEAD, (q + 1) * A_HEAD)
                    states[j] = states[j] * row(w_t, j) + sa[sl] * row(b_t, j) + vb[sl] * row(k_t, j)
                y_in = jnp.concatenate([(states[j] * row(r_t, j)).astype(BF16) for j in members], axis=0)
                yb = _dot(y_in, ones)
                for q, j in enumerate(members):
                    sl = slice(q * A_HEAD, (q + 1) * A_HEAD)
                    y_row = jnp.sum(jnp.where(diag, yb[sl], 0.0), axis=0, keepdims=True)
                    y_tiles[j] = jnp.where(sub == i, y_row, y_tiles[j])
        for j, (g, p) in enumerate(idx):
            s_ref[g, p] = states[j]
            y_ref[g, pl.ds(base, rb), p * LANES:(p + 1) * LANES] = y_tiles[j]
        return carry

    lax.fori_loop(0, tc // rb, block, 0)


def _wkv(seqs, s0, tc, group):
    b, t, d = seqs[0].shape
    pairs = s0.shape[1]
    seq_spec = pl.BlockSpec((group, tc, d), lambda i, j: (i, j, 0))
    st_spec = pl.BlockSpec((group, pairs, A_HEAD, LANES), lambda i, j: (i, 0, 0, 0))
    return pl.pallas_call(
        functools.partial(_wkv_kernel, tc),
        out_shape=[jax.ShapeDtypeStruct((b, t, d), F32), jax.ShapeDtypeStruct(s0.shape, F32)],
        grid=(b // group, t // tc),
        in_specs=[seq_spec] * 6 + [st_spec],
        out_specs=[seq_spec, st_spec],
        compiler_params=_params(("arbitrary", "arbitrary")),
        name="wkv7_scan",
    )(*seqs, s0)


WKV_CHUNK = 64


def _dot_nt(a, b):
    return lax.dot_general(a, b, (((1,), (1,)), ((), ())), preferred_element_type=F32)


def _dot_tn(a, b):
    return lax.dot_general(a, b, (((0,), (0,)), ((), ())), preferred_element_type=F32)


def _wkv_chunk_kernel(r_ref, lw_ref, k_ref, v_ref, a_ref, b_ref, s0_ref, y_ref, s_ref):
    @pl.when(pl.program_id(1) == 0)
    def _():
        s_ref[...] = s0_ref[...]

    groups, pairs = s_ref.shape[0], s_ref.shape[1]
    c = r_ref.shape[1]
    iota = lambda shape, dim: lax.broadcasted_iota(jnp.int32, shape, dim)
    head0 = iota((c, LANES), 1) < A_HEAD
    tri = jnp.where(iota((c, c), 1) <= iota((c, c), 0), 1.0, 0.0).astype(BF16)
    rows4 = iota((4 * c, LANES), 0)
    t_idx = rows4 & (c - 1)
    s_idx = iota((4 * c, LANES), 1) & (c - 1)
    keep = s_idx < t_idx + jnp.where(rows4 < 2 * c, 0, 1)
    same_head = (iota((LANES, LANES), 0) >> 6) == (iota((LANES, LANES), 1) >> 6)
    same_block = (iota((LANES, LANES), 0) >> 3) == (iota((LANES, LANES), 1) >> 3)
    eye = jnp.where(iota((LANES, LANES), 0) == iota((LANES, LANES), 1), 1.0, 0.0)

    def by_head(x):
        return jnp.concatenate([jnp.where(head0, x, 0.0), jnp.where(head0, 0.0, x)], axis=0)

    units = [(g, p) for g in range(groups) for p in range(pairs)]
    each = lambda f, *lists: [f(*args) for args in zip(*lists)]
    bf = lambda xs: [x.astype(BF16) for x in xs]
    swap = lambda x: pltpu.roll(x, A_HEAD, 1)

    def load(ref):
        return [ref[g, :, p * LANES:(p + 1) * LANES].astype(F32) for g, p in units]

    r, lw, k, v, a, b = (load(ref) for ref in (r_ref, lw_ref, k_ref, v_ref, a_ref, b_ref))
    s_old = [s_ref[g, p] for g, p in units]

    def cumulative(x):
        hi, lo = _split(x)
        return _dot(tri, hi) + _dot(tri, lo)

    cum = each(cumulative, lw)
    end = [x[c - 1:c, :] for x in cum]
    inv = [jnp.exp(-x) for x in cum]
    rest = each(lambda e, x: jnp.exp(e - x), end, cum)
    at = each(lambda a_, x, l_: a_ * jnp.exp(x - l_), a, cum, lw)
    rt = each(lambda r_, x: r_ * jnp.exp(x), r, cum)
    bt = each(jnp.multiply, b, inv)
    kt = each(jnp.multiply, k, inv)
    bb = bf(each(jnp.multiply, b, rest))
    kb = bf(each(jnp.multiply, k, rest))
    at_b, rt_b, v_b = bf(at), bf(rt), bf(v)
    lhs = each(lambda x, y: jnp.concatenate([by_head(x), by_head(y)], axis=0).astype(BF16), at, rt)
    rhs = each(lambda x, y: jnp.concatenate([x, y], axis=0).astype(BF16), bt, kt)
    aa = each(lambda x, y: jnp.where(keep, _dot_nt(x, y), 0.0), lhs, rhs)
    a0, a1, r0, r1 = ([x[i * c:(i + 1) * c] for x in aa] for i in range(4))
    n = each(lambda x, y: jnp.concatenate([jnp.where(head0, x, 0.0), jnp.where(head0, 0.0, swap(y))], axis=0), a0, a1)
    a_ak = bf(each(lambda x, y: jnp.where(head0, swap(x), y), a0, a1))
    a_rb = bf(each(lambda x, y: jnp.where(head0, x, swap(y)), r0, r1))
    a_rk = bf(each(lambda x, y: jnp.where(head0, swap(x), y), r0, r1))
    mm = lambda p_, q_: _dot(p_.astype(BF16), q_.astype(BF16))

    def neumann8(z):
        out = [eye + z_ for z_ in z]
        for _ in range(2):
            z = each(mm, z, z)
            out = each(lambda o_, z_: o_ + mm(o_, z_), out, z)
        return out

    d_in = [jnp.where(same_block, n_, 0.0) for n_ in n]
    d_inv = neumann8(d_in)
    m_mat = each(lambda di_, n_, d_: mm(di_, n_ - d_), d_inv, n, d_in)
    inv_t = each(mm, neumann8(m_mat), d_inv)
    v_heads = bf(each(by_head, v))
    s_b = bf(s_old)
    rhs0 = each(lambda at_, s_, ak_, vh_: _dot_nt(at_, s_) + _dot(ak_, vh_), at_b, s_b, a_ak, v_heads)
    uu = each(lambda t_, x_: _dot(t_.astype(BF16), by_head(x_).astype(BF16)), inv_t, rhs0)
    u = [x_[0:c] + x_[c:2 * c] for x_ in uu]
    u_heads = bf(each(by_head, u))
    y = each(lambda rt_, s_, rb_, uh_, rk_, vh_: _dot_nt(rt_, s_) + _dot(rb_, uh_) + _dot(rk_, vh_),
             rt_b, s_b, a_rb, u_heads, a_rk, v_heads)
    upd = each(lambda u_, bb_, v_, kb_: _dot_tn(u_, bb_) + _dot_tn(v_, kb_), bf(u), bb, v_b, kb)
    s_new = each(lambda s_, e, up: s_ * jnp.exp(e) + jnp.where(same_head, up, 0.0), s_old, end, upd)
    for (g, p), y_, s_ in zip(units, y, s_new):
        y_ref[g, :, p * LANES:(p + 1) * LANES] = y_
        s_ref[g, p] = s_


def _wkv_chunked(seqs, s0, group):
    b, t, d = seqs[0].shape
    pairs = s0.shape[1]
    seq_spec = pl.BlockSpec((group, WKV_CHUNK, d), lambda i, j: (i, j, 0))
    st_spec = pl.BlockSpec((group, pairs, LANES, LANES), lambda i, j: (i, 0, 0, 0))
    return pl.pallas_call(
        _wkv_chunk_kernel,
        out_shape=[jax.ShapeDtypeStruct((b, t, d), F32), jax.ShapeDtypeStruct(s0.shape, F32)],
        grid=(b // group, t // WKV_CHUNK),
        in_specs=[seq_spec] * 6 + [st_spec],
        out_specs=[seq_spec, st_spec],
        compiler_params=_params(("arbitrary", "arbitrary")),
        name="wkv7_chunked",
    )(*seqs, s0)


def _pair_block_diag(s):
    b, h, n, _ = s.shape
    s = s.reshape(b, h // 2, 2, n, n)
    z = jnp.zeros_like(s[:, :, 0])
    top = jnp.concatenate([s[:, :, 0], z], axis=-1)
    bot = jnp.concatenate([z, s[:, :, 1]], axis=-1)
    return jnp.concatenate([top, bot], axis=-2)


def _pair_diag_blocks(s):
    b, p, _, _ = s.shape
    n = A_HEAD
    return jnp.stack([s[:, :, 0:n, 0:n], s[:, :, n:, n:]], axis=2).reshape(b, 2 * p, n, n)


def _mix_out_kernel(rwkv, alpha, d, *refs):
    if rwkv:
        y_ref, g_ref, bonus_ref, x_ref, mod_ref, wo_ref, gn_ref, ln_ref, o_ref = refs
        y = y_ref[...]
        ones = _pair_ones()
        mu_y = _head_sum(y, ones) * (1.0 / A_HEAD)
        dy = y - mu_y
        var_y = _head_sum(dy * dy, ones) * (1.0 / A_HEAD)
        yn = dy * lax.rsqrt(var_y + GN_EPS) * gn_ref[0:1, :] + gn_ref[1:2, :]
        pre = ((yn + bonus_ref[...].astype(F32)) * g_ref[...].astype(F32)).astype(BF16)
    else:
        on_ref, x_ref, mod_ref, wo_ref, ln_ref, o_ref = refs
        pre = on_ref[...].astype(BF16)
    mix = _dot(pre, wo_ref[...])
    gt = mod_ref[0][:, 2 * d:3 * d]
    z = alpha * x_ref[...] + (1.0 + gt) * mix
    o_ref[...] = _layer_norm(z, ln_ref[0:1, :], ln_ref[1:2, :])


def _mix_out(rwkv, alpha, ins, x2d, mod, tiles_per_batch, tm, wo, gn, ln):
    n, d = x2d.shape
    row = pl.BlockSpec((tm, d), lambda i: (i, 0))
    extra = [gn] if rwkv else []
    args = list(ins) + [x2d, mod, wo] + extra + [ln]
    specs = [row] * (len(ins) + 1) + [_mod_spec(mod, tiles_per_batch), _full(wo)] + [_full(e) for e in extra] + [_full(ln)]
    return pl.pallas_call(
        functools.partial(_mix_out_kernel, rwkv, alpha, d),
        out_shape=jax.ShapeDtypeStruct((n, d), F32),
        grid=(n // tm,),
        in_specs=specs,
        out_specs=row,
        compiler_params=_params(("arbitrary",)),
        name="rwkv_out_ln" if rwkv else "attn_out_ln",
    )(*args)


def _mlp_kernel(alpha, d, x_ref, mod_ref, w1_ref, w2_ref, ln_ref, o_ref, hf_ref, acc_ref):
    j = pl.program_id(1)
    mod = mod_ref[0]

    @pl.when(j == 0)
    def _():
        hf_ref[...] = (x_ref[...] * (1.0 + mod[:, 4 * d:5 * d]) + mod[:, 3 * d:4 * d]).astype(BF16)
        acc_ref[...] = jnp.zeros_like(acc_ref)

    h1 = jnp.maximum(_dot(hf_ref[...], w1_ref[...]), 0.0)
    acc_ref[...] += _dot((h1 * h1).astype(BF16), w2_ref[...])

    @pl.when(j == pl.num_programs(1) - 1)
    def _():
        z = alpha * x_ref[...] + (1.0 + mod[:, 5 * d:6 * d]) * acc_ref[...]
        o_ref[...] = _layer_norm(z, ln_ref[0:1, :], ln_ref[1:2, :])


def _mlp(alpha, x2d, mod, tiles_per_batch, tm, w1, w2, ln):
    n, d = x2d.shape
    dff = w1.shape[1]
    tf = 1024
    row = pl.BlockSpec((tm, d), lambda i, j: (i, 0))
    return pl.pallas_call(
        functools.partial(_mlp_kernel, alpha, d),
        out_shape=jax.ShapeDtypeStruct((n, d), F32),
        grid=(n // tm, dff // tf),
        in_specs=[row, _mod_spec(mod, tiles_per_batch),
                  pl.BlockSpec((d, tf), lambda i, j: (0, j)),
                  pl.BlockSpec((tf, d), lambda i, j: (j, 0)),
                  _full(ln)],
        out_specs=row,
        scratch_shapes=[pltpu.VMEM((tm, d), BF16), pltpu.VMEM((tm, d), F32)],
        compiler_params=_params(("arbitrary", "arbitrary")),
        name="sq_relu_mlp_ln",
    )(x2d, mod, w1, w2, ln)


def _qkv_kernel(d, scale, transposed, x_ref, kvmod_ref, mod_ref, wk_ref, wv_ref, wq_ref, k_o, v_o, *extra):
    x = x_ref[...]
    kvmod = kvmod_ref[0]
    hkv = (x * (1.0 + kvmod[:, d:2 * d]) + kvmod[:, 0:d]).astype(BF16)
    k = _dot(hkv, wk_ref[...])
    v = _dot(hkv, wv_ref[...])
    v_o[...] = v
    mod = mod_ref[0]
    h = (x * (1.0 + mod[:, d:2 * d]) + mod[:, 0:d]).astype(BF16)
    q = _dot(h, wq_ref[...]) * scale
    if transposed:
        kb_o, vt_o, qt_o = extra
        k_o[0] = k.T
        kb_o[...] = k.astype(BF16)
        vt_o[0] = v.T.astype(BF16)
        qt_o[0] = q.T.astype(BF16)
    else:
        (q_o,) = extra
        k_o[...] = k
        q_o[...] = q


def _qkv(x2d, kvmod, mod, tiles_per_batch, tm, wk, wv, wq, scale, transposed):
    n, d = x2d.shape
    row = pl.BlockSpec((tm, d), lambda i: (i, 0))
    row_sds = jax.ShapeDtypeStruct((n, d), F32)
    if transposed:
        nb = n // (tiles_per_batch * tm)
        col = pl.BlockSpec((1, d, tm), lambda i: (i // tiles_per_batch, 0, i % tiles_per_batch))
        col_shape = (nb, d, tiles_per_batch * tm)
        t_sds = jax.ShapeDtypeStruct(col_shape, BF16)
        out_shape = [jax.ShapeDtypeStruct(col_shape, F32), row_sds, jax.ShapeDtypeStruct((n, d), BF16), t_sds, t_sds]
        out_specs = [col, row, row, col, col]
    else:
        out_shape = [row_sds] * 3
        out_specs = [row] * 3
    return pl.pallas_call(
        functools.partial(_qkv_kernel, d, scale, transposed),
        out_shape=out_shape,
        grid=(n // tm,),
        in_specs=[row, _mod_spec(kvmod, tiles_per_batch), _mod_spec(mod, tiles_per_batch), _full(wk), _full(wv), _full(wq)],
        out_specs=out_specs,
        compiler_params=_params(("arbitrary",)),
        name="qkv_proj",
    )(x2d, kvmod, mod, wk, wv, wq)


def _diff_lambda(lq_ref, lk_ref, lam_init):
    lq = lq_ref[...]
    lk = lk_ref[...]
    dots = jnp.sum(lq * lk, -1, keepdims=True)
    return jnp.exp(dots[0:1, :]) - jnp.exp(dots[1:2, :]) + lam_init


DENOM_ROWS = 16


def _attn_kernel(tq, lam_init, qi_tab, ki_tab, qt_ref, k_ref, vt_ref, lq_ref, lk_ref, g_ref, o_ref,
                 qs_ref, m_ref, acc_ref):
    t = pl.program_id(2)
    qi = qi_tab[t]
    ki = ki_tab[t]

    heads_here = qs_ref.shape[0]
    head = lambda e: slice(e * LANES, (e + 1) * LANES)

    @pl.when(ki == 0)
    def _():
        for e in range(heads_here):
            qt = qt_ref[0, head(e), :]
            row = lax.broadcasted_iota(jnp.int32, qt.shape, 0)
            qs_ref[e, :, 0:tq] = jnp.where(row < B_HEAD, qt, jnp.zeros_like(qt))
            qs_ref[e, :, tq:2 * tq] = jnp.where(row >= B_HEAD, qt, jnp.zeros_like(qt))
        m_ref[...] = jnp.full(m_ref.shape, NEG, F32)
        acc_ref[...] = jnp.zeros_like(acc_ref)

    def update(diagonal):
        for e in range(heads_here):
            s = _dot(k_ref[0, :, head(e)], qs_ref[e])
            if diagonal:
                kpos = lax.broadcasted_iota(jnp.int32, s.shape, 0)
                qpos = lax.broadcasted_iota(jnp.int32, s.shape, 1) & (tq - 1)
                s = jnp.where(kpos <= qpos, s, NEG)
            m_prev = m_ref[e]
            m_new = jnp.maximum(m_prev, jnp.max(s, 0, keepdims=True))
            alpha = jnp.exp2(m_prev - m_new)
            p = jnp.exp2(s - m_new).astype(BF16)
            vt_ones = jnp.concatenate([vt_ref[0, head(e), :], jnp.ones((DENOM_ROWS, vt_ref.shape[2]), BF16)], axis=0)
            acc_ref[e] = alpha * acc_ref[e] + _dot(vt_ones, p)
            m_ref[e] = m_new

    @pl.when(ki < qi)
    def _():
        update(False)

    @pl.when(ki == qi)
    def _():
        update(True)
        lam = _diff_lambda(lq_ref, lk_ref, lam_init)
        for e in range(heads_here):
            o_all = acc_ref[e, 0:LANES, :] * (1.0 / acc_ref[e, LANES:LANES + 1, :])
            o = o_all[:, 0:tq] - lam * o_all[:, tq:2 * tq]
            on = o * lax.rsqrt(jnp.mean(o * o, 0, keepdims=True) + SUBLN_EPS) * g_ref[...] * (1.0 - lam_init)
            o_ref[0, :, head(e)] = on.T.astype(o_ref.dtype)


def _diff_attn_prompt(qt, k, vt, lam_q, lam_k, g_col, lam_init, tq):
    b, s, d = k.shape
    heads = d // LANES
    nq = s // tq
    pairs = [(qi, ki) for qi in range(nq) for ki in range(qi + 1)]
    qi_tab = jnp.asarray([p[0] for p in pairs], jnp.int32)
    ki_tab = jnp.asarray([p[1] for p in pairs], jnp.int32)
    hp = 4 if heads % 4 == 0 else 1
    width = hp * LANES
    qt_spec = pl.BlockSpec((1, width, tq), lambda bi, h, t, qt_, kt_: (bi, h, qt_[t]))
    vt_spec = pl.BlockSpec((1, width, tq), lambda bi, h, t, qt_, kt_: (bi, h, kt_[t]))
    k_spec = pl.BlockSpec((1, tq, width), lambda bi, h, t, qt_, kt_: (bi, kt_[t], h))
    o_spec = pl.BlockSpec((1, tq, width), lambda bi, h, t, qt_, kt_: (bi, qt_[t], h))
    small = lambda a: pl.BlockSpec(a.shape, lambda *_: (0,) * a.ndim)
    grid_spec = pltpu.PrefetchScalarGridSpec(
        num_scalar_prefetch=2,
        grid=(b, heads // hp, len(pairs)),
        in_specs=[qt_spec, k_spec, vt_spec, small(lam_q), small(lam_k), small(g_col)],
        out_specs=o_spec,
        scratch_shapes=[pltpu.VMEM((hp, LANES, 2 * tq), BF16), pltpu.VMEM((hp, 1, 2 * tq), F32),
                        pltpu.VMEM((hp, LANES + DENOM_ROWS, 2 * tq), F32)],
    )
    return pl.pallas_call(
        functools.partial(_attn_kernel, tq, lam_init),
        out_shape=jax.ShapeDtypeStruct((b, s, d), BF16),
        grid_spec=grid_spec,
        compiler_params=_params(("arbitrary",) * 3),
        name="diff_attn_prompt",
    )(qi_tab, ki_tab, qt, k, vt, lam_q, lam_k, g_col)


def _decode_kernel(pp, lam_init, heads, pt_ref, qcol_ref, qrow_ref, kn_ref, vn_ref, lq_ref, lk_ref, g_ref, *rest):
    k_refs = rest[:pp]
    v_refs = rest[pp:2 * pp]
    o_ref, qb_ref, m_ref, l_ref, acc_ref = rest[2 * pp:]
    step = pl.program_id(1)
    rows = 2 * heads
    d = qb_ref.shape[0]

    @pl.when(step == 0)
    def _():
        qb_ref[...] = jnp.broadcast_to(qcol_ref[0], qb_ref.shape)
        m_ref[...] = jnp.full(m_ref.shape, NEG, F32)
        l_ref[...] = jnp.zeros_like(l_ref)
        acc_ref[...] = jnp.zeros_like(acc_ref)

    r = lax.broadcasted_iota(jnp.int32, (rows, d), 0)
    c = lax.broadcasted_iota(jnp.int32, (rows, d), 1)
    sel = ((c >> 7) == (r & (heads - 1))) & (((c >> 6) & 1) == (r // heads))
    sel_b = jnp.where(sel, 1.0, 0.0).astype(BF16)
    row_head = lax.broadcasted_iota(jnp.int32, (rows, LANES), 0) & (heads - 1)

    s = jnp.concatenate([_dot(sel_b, (k_refs[i][0] * qb_ref[...]).astype(BF16)) for i in range(pp)], axis=1)
    m_prev = m_ref[...]
    m_new = jnp.maximum(m_prev, jnp.max(s, -1, keepdims=True))
    alpha = jnp.exp(m_prev - m_new)
    p = jnp.exp(s - m_new)
    l_ref[...] = alpha * l_ref[...] + jnp.sum(p, -1, keepdims=True)
    pb = p.astype(BF16)
    pv = jnp.zeros((rows, LANES), F32)
    for i in range(pp):
        pb_i = pb[:, i * PAGE_SIZE:(i + 1) * PAGE_SIZE]
        for h in range(heads):
            v_h = v_refs[i][0, pl.ds(h, PAGE_SIZE, stride=heads), :].astype(BF16)
            pv = pv + _dot(jnp.where(row_head == h, pb_i, jnp.zeros_like(pb_i)), v_h)
    acc_ref[...] = alpha * acc_ref[...] + pv
    m_ref[...] = m_new

    @pl.when(step == pl.num_programs(1) - 1)
    def _():
        s_new = jnp.sum(jnp.where(sel, qrow_ref[0] * kn_ref[0], 0.0), -1, keepdims=True)
        m_prev = m_ref[...]
        m_fin = jnp.maximum(m_prev, s_new)
        alpha = jnp.exp(m_prev - m_fin)
        p_new = jnp.exp(s_new - m_fin)
        l_fin = alpha * l_ref[...] + p_new
        o_all = (alpha * acc_ref[...] + p_new * vn_ref[0]) / l_fin
        lam = _diff_lambda(lq_ref, lk_ref, lam_init)
        o = o_all[0:heads] - lam * o_all[heads:rows]
        on = o * lax.rsqrt(jnp.mean(o * o, -1, keepdims=True) + SUBLN_EPS) * g_ref[...] * (1.0 - lam_init)
        o_ref[0] = on


def _diff_attn_decode(q, k_new, v_new, cache_kt, cache_v, page_table, lam_q, lam_k, g_row, lam_init, pp):
    bd, d = q.shape
    heads = d // LANES
    rows = 2 * heads
    n_pages = page_table.shape[1]
    at_b = lambda shape: pl.BlockSpec((1,) + shape, lambda b, s, pt: (b, 0, 0))

    def page_spec(shape, i):
        return pl.BlockSpec((1,) + shape, lambda b, s, pt: (pt[b, s * pp + i], 0, 0))

    small = lambda a: pl.BlockSpec(a.shape, lambda b, s, pt: (0,) * a.ndim)
    grid_spec = pltpu.PrefetchScalarGridSpec(
        num_scalar_prefetch=1,
        grid=(bd, n_pages // pp),
        in_specs=[at_b((d, 1)), at_b((1, d)), at_b((1, d)), at_b((rows, LANES)), small(lam_q), small(lam_k), small(g_row)]
                 + [page_spec((d, PAGE_SIZE), i) for i in range(pp)]
                 + [page_spec((PAGE_SIZE * heads, LANES), i) for i in range(pp)],
        out_specs=at_b((heads, LANES)),
        scratch_shapes=[pltpu.VMEM((d, PAGE_SIZE), F32), pltpu.VMEM((rows, 1), F32), pltpu.VMEM((rows, 1), F32),
                        pltpu.VMEM((rows, LANES), F32)],
    )
    v_rows = jnp.tile(v_new, (1, 2, 1))
    return pl.pallas_call(
        functools.partial(_decode_kernel, pp, lam_init, heads),
        out_shape=jax.ShapeDtypeStruct((bd, heads, LANES), F32),
        grid_spec=grid_spec,
        compiler_params=_params(("arbitrary", "arbitrary")),
        name="diff_attn_decode",
    )(page_table, q.reshape(bd, d, 1), q.reshape(bd, 1, d), k_new.reshape(bd, 1, d), v_rows, lam_q, lam_k, g_row,
      *([cache_kt] * pp), *([cache_v] * pp))


def _pad_to(a, axis, size):
    pad = [(0, 0)] * a.ndim
    pad[axis] = (0, size - a.shape[axis])
    return jnp.pad(a, pad)


def _pack_state(s):
    b, h, n, _ = s.shape
    return s.reshape(b, h // 2, 2, n, n).transpose(0, 1, 3, 2, 4).reshape(b, h // 2, n, 2 * n)


def _unpack_state(s):
    b, p, n, _ = s.shape
    return s.reshape(b, p, n, 2, n).transpose(0, 1, 3, 2, 4).reshape(b, 2 * p, n, n)


def kernel(x_prompt, x_sample, cache_k, cache_v, state_wkv, state_shift, page_table, c_prompt, c_sample, ln_g, ln_b, ada_w, ada_b, ffn_w1, ffn_w2, a_mu, a_w_rkv, a_w_o, a_w0, a_w1, a_w2, a_a0, a_a1, a_a2, a_g1, a_g2, a_k_k, a_k_a, a_r_k, a_gn_g, a_gn_b, kv_ada_w, kv_ada_b, kv_w_k, kv_w_v, b_w_q, b_w_o, b_lam_q, b_lam_k, b_subln_g):
    bp, seq, d = x_prompt.shape
    bd, dec_seq, _ = x_sample.shape
    depth = ada_w.shape[0]
    n_a = a_mu.shape[0]
    assert depth == 2 and n_a == 1 and dec_seq == 1
    alpha = (2 * depth) ** 0.25
    heads = d // LANES
    pool = cache_k.shape[0]

    n_c = bp + bd
    c_all = _pad_to(jnp.concatenate([c_prompt, c_sample], axis=0), 0, -(-n_c // 8) * 8)
    mods = [_ada(c_all, ada_w, ada_b.reshape(depth, 1, -1), l) for l in range(depth)]
    kvmods = _ada(c_all, kv_ada_w[None], kv_ada_b.reshape(1, 1, -1), 0)

    wrkv = a_w_rkv[0].astype(BF16)
    lora = 128
    w1 = _pad_to(a_w1[0], 1, lora).astype(BF16)
    w2 = _pad_to(a_w2[0], 0, lora).astype(BF16)
    a1 = _pad_to(a_a1[0], 1, lora).astype(BF16)
    a2 = _pad_to(a_a2[0], 0, lora).astype(BF16)
    g1 = _pad_to(a_g1[0], 1, 2 * lora).astype(BF16)
    g2 = _pad_to(a_g2[0], 0, 2 * lora).astype(BF16)
    vec = _pad_to(jnp.stack([a_w0[0], a_a0[0], a_k_k[0], a_k_a[0], a_r_k[0].reshape(d)]), 0, 8)
    gn = jnp.stack([a_gn_g[0], a_gn_b[0]])
    a_wo = a_w_o[0].astype(BF16)
    ffn1 = ffn_w1.astype(BF16)
    ffn2 = ffn_w2.astype(BF16)
    wk = kv_w_k.astype(BF16)
    wv = kv_w_v.astype(BF16)
    wq = b_w_q[0].astype(BF16)
    b_wo = b_w_o[0].astype(BF16)
    g_head = b_subln_g[0].reshape(1, LANES)
    lam_init = 0.8 - 0.6 * math.exp(-0.3 * n_a)
    ln = [[jnp.stack([ln_g[l, s], ln_b[l, s]]) for s in range(2)] for l in range(depth)]
    attn_scale = B_HEAD ** -0.5

    def trunk(x, rows, shift0, wkv0, seq_len, tm, tm_mlp, tc, attend):
        b = x.shape[0]
        x2d = x.reshape(b * seq_len, d)
        tiles = max(seq_len // tm, 1)
        tiles_mlp = max(seq_len // tm_mlp, 1)
        if seq_len == 1:
            shape_mod = lambda m: m[rows].reshape(1, b, -1)
            sh0 = shift0.reshape(1, b, d)
        else:
            shape_mod = lambda m: m[rows].reshape(b, 1, -1)
            sh0 = shift0.reshape(b, 1, d)
        mod0, mod1, kvmod = shape_mod(mods[0]), shape_mod(mods[1]), shape_mod(kvmods)

        *seqs, g, bonus, last = _rwkv_proj(x2d, mod0, sh0, seq_len, tm, a_mu[0], wrkv, w1, w2, a1, a2, g1, g2, vec)
        seqs = [s.reshape(b, seq_len, d) for s in seqs]
        if seq_len == 1:
            y, s_fin = _wkv(seqs, _pack_state(wkv0), tc, 2)
            s_fin = _unpack_state(s_fin)
        else:
            y, s_fin = _wkv_chunked(seqs, _pair_block_diag(wkv0), 2)
            s_fin = _pair_diag_blocks(s_fin)
        x1 = _mix_out(True, alpha, [y.reshape(b * seq_len, d), g, bonus], x2d, mod0, tiles, tm, a_wo, gn, ln[0][0])
        x2 = _mlp(alpha, x1, mod0, tiles_mlp, tm_mlp, ffn1[0], ffn2[0], ln[0][1])
        if seq_len > 1:
            kt, v, kb, vt, qt = _qkv(x2, kvmod, mod1, tiles, tm, wk, wv, wq, attn_scale * LOG2E, True)
            k_out = kt.reshape(b, heads, 2, B_HEAD, seq_len).transpose(0, 4, 1, 2, 3)
            on = attend(kb, vt, qt)
        else:
            k, v, q = _qkv(x2, kvmod, mod1, tiles, tm, wk, wv, wq, attn_scale, False)
            k_out = k.reshape(b, seq_len, heads, 2, B_HEAD)
            on = attend(k, v, q)
        x3 = _mix_out(False, alpha, [on], x2, mod1, tiles, tm, b_wo, gn, ln[1][0])
        y_out = _mlp(alpha, x3, mod1, tiles_mlp, tm_mlp, ffn1[1], ffn2[1], ln[1][1])
        return (y_out.reshape(b, seq_len, d), k_out, v.reshape(b, seq_len, heads, 2 * B_HEAD),
                s_fin[None], last.reshape(1, b, d))

    def attend_prompt(kb, vt, qt):
        on = _diff_attn_prompt(qt, kb.reshape(bp, seq, d), vt, b_lam_q[0], b_lam_k[0], g_head.reshape(LANES, 1),
                               lam_init, min(1024, seq))
        return on.reshape(bp * seq, d)

    def attend_sample(k, v, q):
        cache_kt = jnp.transpose(cache_k, (0, 2, 3, 4, 1)).reshape(pool, d, PAGE_SIZE)
        on = _diff_attn_decode(q, k, v.reshape(bd, heads, LANES), cache_kt,
                               cache_v.reshape(pool, PAGE_SIZE * heads, LANES), page_table,
                               b_lam_q[0], b_lam_k[0], g_head, lam_init, min(8, page_table.shape[1]))
        return on.reshape(bd, d)

    zeros_shift = jnp.zeros((bp, d), F32)
    zeros_wkv = jnp.zeros((bp, d // A_HEAD, A_HEAD, A_HEAD), F32)
    y_p, k_p, v_p, wkv_p, shift_p = trunk(x_prompt, slice(0, bp), zeros_shift, zeros_wkv, seq,
                                          min(512, seq), min(1024, seq), min(128, seq), attend_prompt)
    y_s, k_s, v_s, wkv_s, shift_s = trunk(x_sample, slice(bp, bp + bd), state_shift[0], state_wkv[0], 1,
                                          bd, bd, 1, attend_sample)
    return (y_p, y_s, k_p, v_p, wkv_p, shift_p, k_s, v_s, wkv_s, shift_s)
```
